```python
import jax, jax.numpy as jnp
from jax import lax
import numpy as np

D_MODEL = 1024
BATCH = 8
SEQ = 2048
DEPTH = 2

N_A = DEPTH // 2
N_B = DEPTH - N_A
RET_HEADS = 4
RET_QK_DIM = D_MODEL // RET_HEADS
RET_V_DIM = 2 * D_MODEL // RET_HEADS
RET_CHUNK = 128
ROPE_BASE = 10000.0
RET_IN_WIDTH = 2 * RET_HEADS * RET_QK_DIM + 2 * RET_HEADS * RET_V_DIM
FOX_HEADS = 16
FOX_HEAD_DIM = D_MODEL // FOX_HEADS
Q_BLOCK = 128
FORGET_BIAS_CENTER = 2.0
D_FF = 4 * D_MODEL
EPS = 1e-6
MAX_POS_OFFSET = 4096

kernel_name = "yoco_retention_forgetting_attention_adaln"


def rms_norm(x, gain):
    xf = x.astype(jnp.float32)
    y = xf * lax.rsqrt(jnp.mean(xf * xf, axis=-1, keepdims=True) + EPS)
    return (y * gain.astype(jnp.float32)).astype(x.dtype)


def ada_modulate(x, gain, shift, scale):
    return rms_norm(x, gain) * (1 + scale[:, None, :]) + shift[:, None, :]


def rotary(x, positions):
    half = x.shape[-1] // 2
    inv_freq = ROPE_BASE ** (-jnp.arange(half, dtype=jnp.float32) / half)
    ang = positions.astype(jnp.float32)[..., None] * inv_freq
    cos = jnp.cos(ang)[:, :, None, :]
    sin = jnp.sin(ang)[:, :, None, :]
    xf = x.astype(jnp.float32)
    x1, x2 = xf[..., :half], xf[..., half:]
    out = jnp.concatenate([x1 * cos - x2 * sin, x1 * sin + x2 * cos], axis=-1)
    return out.astype(x.dtype)


def retention(q, k, v, gammas):
    B, S, H, Dk = q.shape
    Dv = v.shape[-1]
    C = RET_CHUNK
    NC = S // C
    dt = q.dtype
    k = k * (Dk ** -0.5)
    qc = q.reshape(B, NC, C, H, Dk).transpose(1, 0, 3, 2, 4)
    kc = k.reshape(B, NC, C, H, Dk).transpose(1, 0, 3, 2, 4)
    vc = v.reshape(B, NC, C, H, Dv).transpose(1, 0, 3, 2, 4)
    log_g = jnp.log(gammas.astype(jnp.float32))
    idx = jnp.arange(C, dtype=jnp.float32)
    rel = idx[:, None] - idx[None, :]
    intra_decay = jnp.where(rel >= 0, jnp.exp(log_g[:, None, None] * jnp.maximum(rel, 0.0)), 0.0)
    q_decay = jnp.exp(log_g[:, None] * (idx + 1.0))
    k_decay = jnp.exp(log_g[:, None] * (C - 1.0 - idx))
    chunk_decay = jnp.exp(log_g * C)
    scores = jnp.einsum('nbhqd,nbhkd->nbhqk', qc, kc) * intra_decay
    intra = jnp.einsum('nbhqk,nbhkv->nbhqv', scores.astype(dt), vc).astype(jnp.float32)

    def step(state, inp):
        q_i, k_i, v_i = inp
        cross = jnp.einsum('bhqd,bhdv->bhqv', q_i.astype(jnp.float32), state) * q_decay[None, :, :, None]
        upd = jnp.einsum('bhkd,bhkv->bhdv', (k_i.astype(jnp.float32) * k_decay[None, :, :, None]),
                         v_i.astype(jnp.float32))
        return state * chunk_decay[None, :, None, None] + upd, cross

    state0 = jnp.zeros((B, H, Dk, Dv), jnp.float32)
    _, cross = lax.scan(step, state0, (qc, kc, vc))
    out = intra + cross
    return out.transpose(1, 0, 3, 2, 4).reshape(B, S, H, Dv).astype(dt)


def retention_mixer(h, positions, w_in, norm_gain, w_out, gammas):
    B, S, _ = h.shape
    proj = h @ w_in
    qk_w = RET_HEADS * RET_QK_DIM
    v_w = RET_HEADS * RET_V_DIM
    q, k, v, g = jnp.split(proj, [qk_w, 2 * qk_w, 2 * qk_w + v_w], axis=-1)
    q = rotary(q.reshape(B, S, RET_HEADS, RET_QK_DIM), positions)
    k = rotary(k.reshape(B, S, RET_HEADS, RET_QK_DIM), positions)
    v = v.reshape(B, S, RET_HEADS, RET_V_DIM)
    y = retention(q, k, v, gammas)
    y = rms_norm(y, norm_gain).reshape(B, S, v_w)
    return (jax.nn.silu(g) * y) @ w_out


def shared_kv(x, c_act, kv_norm_gain, kv_w_ada, kv_b_ada, kv_w, forget_bias, k_norm_gain):
    B, S, D = x.shape
    shift, scale = jnp.split(c_act @ kv_w_ada + kv_b_ada, 2, axis=-1)
    h = ada_modulate(x, kv_norm_gain, shift, scale)
    kvf = h @ kv_w
    k, v, f = jnp.split(kvf, [D, 2 * D], axis=-1)
    k = rms_norm(k.reshape(B, S, FOX_HEADS, FOX_HEAD_DIM), k_norm_gain)
    v = v.reshape(B, S, FOX_HEADS, FOX_HEAD_DIM)
    log_f = jax.nn.log_sigmoid(f.astype(jnp.float32) + forget_bias.astype(jnp.float32))
    f_cum = jnp.cumsum(log_f, axis=1)
    return k, v, f_cum


def forgetting_attention(q, k, v, f_cum):
    B, S, H, Dh = q.shape
    NB = S // Q_BLOCK
    scale = Dh ** -0.5
    k_pos = jnp.arange(S)
    fk = f_cum.transpose(0, 2, 1)
    qb = q.reshape(B, NB, Q_BLOCK, H, Dh).transpose(1, 0, 3, 2, 4)
    fq = fk.reshape(B, H, NB, Q_BLOCK).transpose(2, 0, 1, 3)

    def block(args):
        q_blk, fq_blk, blk = args
        logits = jnp.einsum('bhqd,bkhd->bhqk', q_blk, k).astype(jnp.float32) * scale
        logits = logits + fq_blk[..., None] - fk[:, :, None, :]
        q_pos = blk * Q_BLOCK + jnp.arange(Q_BLOCK)
        causal = k_pos[None, :] <= q_pos[:, None]
        logits = jnp.where(causal, logits, -jnp.inf)
        p = jax.nn.softmax(logits, axis=-1).astype(v.dtype)
        return jnp.einsum('bhqk,bkhd->bqhd', p, v)

    out = lax.map(block, (qb, fq, jnp.arange(NB)))
    return out.transpose(1, 0, 2, 3, 4).reshape(B, S, H, Dh)


def fox_mixer(h, k, v, f_cum, w_in, q_norm_gain, w_out):
    B, S, D = h.shape
    q, og = jnp.split(h @ w_in, 2, axis=-1)
    q = rms_norm(q.reshape(B, S, FOX_HEADS, FOX_HEAD_DIM), q_norm_gain)
    y = forgetting_attention(q, k, v, f_cum).reshape(B, S, D)
    return (jax.nn.sigmoid(og) * y) @ w_out


def sq_relu_mlp(h, w1, w2):
    return jnp.square(jax.nn.relu(h @ w1)) @ w2


def setup_inputs(seed: int = 0) -> dict:
    key = jax.random.key(seed)
    ks = jax.random.split(key, 24)
    f32 = jnp.float32
    D = D_MODEL

    def w(k, shape, fan_in, s=1.0):
        return jax.random.normal(k, shape, f32) * (s * fan_in ** -0.5)

    def gain(k, shape):
        return 1.0 + 0.02 * jax.random.normal(k, shape, f32)

    x = jax.random.normal(ks[0], (BATCH, SEQ, D), f32)
    c = jax.random.normal(ks[1], (BATCH, D), f32)
    offset = jax.random.randint(ks[2], (BATCH, 1), 0, MAX_POS_OFFSET, dtype=jnp.int32)
    positions = offset + jnp.arange(SEQ, dtype=jnp.int32)[None, :]
    return {
        "x": x,
        "c": c,
        "positions": positions,
        "norm_mix_gain": gain(ks[3], (DEPTH, D)),
        "norm_mlp_gain": gain(ks[4], (DEPTH, D)),
        "w_ada": w(ks[5], (DEPTH, D, 6 * D), D, 0.5),
        "b_ada": 0.02 * jax.random.normal(ks[6], (DEPTH, 6 * D), f32),
        "w_mlp_in": w(ks[7], (DEPTH, D, D_FF), D),
        "w_mlp_out": w(ks[8], (DEPTH, D_FF, D), D_FF),
        "ret_w_in": w(ks[9], (N_A, D, RET_IN_WIDTH), D),
        "ret_norm_gain": gain(ks[10], (N_A, RET_HEADS, RET_V_DIM)),
        "ret_w_out": w(ks[11], (N_A, RET_HEADS * RET_V_DIM, D), RET_HEADS * RET_V_DIM),
        "kv_norm_gain": gain(ks[12], (D,)),
        "kv_w_ada": w(ks[13], (D, 2 * D), D, 0.5),
        "kv_b_ada": 0.02 * jax.random.normal(ks[14], (2 * D,), f32),
        "kv_w": w(ks[15], (D, 2 * D + FOX_HEADS), D),
        "forget_bias": FORGET_BIAS_CENTER + 0.1 * jax.random.normal(ks[16], (FOX_HEADS,), f32),
        "k_norm_gain": gain(ks[17], (FOX_HEAD_DIM,)),
        "fox_w_in": w(ks[18], (N_B, D, 2 * D), D),
        "q_norm_gain": gain(ks[19], (N_B, FOX_HEAD_DIM)),
        "fox_w_out": w(ks[20], (N_B, D, D), D),
    }


def reference(x, c, positions, norm_mix_gain, norm_mlp_gain, w_ada, b_ada, w_mlp_in, w_mlp_out,
              ret_w_in, ret_norm_gain, ret_w_out, kv_norm_gain, kv_w_ada, kv_b_ada, kv_w,
              forget_bias, k_norm_gain, fox_w_in, q_norm_gain, fox_w_out):
    c_act = jax.nn.silu(c)
    gammas = 1.0 - jnp.power(2.0, -5.0 - jnp.arange(RET_HEADS, dtype=jnp.float32))
    k_sh = v_sh = f_sh = None
    for layer in range(DEPTH):
        if layer == N_A:
            k_sh, v_sh, f_sh = shared_kv(x, c_act, kv_norm_gain, kv_w_ada, kv_b_ada, kv_w,
                                         forget_bias, k_norm_gain)
        ada = c_act @ w_ada[layer] + b_ada[layer]
        sh1, sc1, g1, sh2, sc2, g2 = jnp.split(ada, 6, axis=-1)
        h = ada_modulate(x, norm_mix_gain[layer], sh1, sc1)
        if layer < N_A:
            mix = retention_mixer(h, positions, ret_w_in[layer], ret_norm_gain[layer],
                                  ret_w_out[layer], gammas)
        else:
            j = layer - N_A
            mix = fox_mixer(h, k_sh, v_sh, f_sh, fox_w_in[j], q_norm_gain[j], fox_w_out[j])
        x = x + g1[:, None, :] * mix
        h = ada_modulate(x, norm_mlp_gain[layer], sh2, sc2)
        x = x + g2[:, None, :] * sq_relu_mlp(h, w_mlp_in[layer], w_mlp_out[layer])
    return x
```

```python
import functools

import jax
import jax.numpy as jnp
import numpy as np
from jax import lax
from jax.experimental import pallas as pl
from jax.experimental.pallas import tpu as pltpu

F32 = jnp.float32
BF = jnp.bfloat16

D_MODEL = 1024
RET_HEADS = 4
RET_QK = 256
RET_V = 512
RET_CHUNK = 128
FOX_HEADS = 16
FOX_DH = 64
D_FF = 4096
EPS = 1e-6
ROPE_BASE = 10000.0
LANES = 128
MXU_DIM = 256
VMEM_LIMIT = 48 * 1024 * 1024


def _params(sem):
    return pltpu.CompilerParams(dimension_semantics=sem, vmem_limit_bytes=VMEM_LIMIT)


def _resident(shape, index_map):
    return pl.BlockSpec(shape, index_map, pipeline_mode=pl.Buffered(1))


def _ln_mod(x, gain, shift, scale):
    ms = jnp.mean(x * x, axis=-1, keepdims=True)
    y = x * lax.rsqrt(ms + EPS)
    return (y * gain) * (1.0 + scale) + shift


def _dot(a, b):
    return jnp.dot(a, b, preferred_element_type=F32)


def _dot_nt(a, b):
    return lax.dot_general(a, b, (((1,), (1,)), ((), ())), preferred_element_type=F32)


def _ada_kernel(c_ref, w_ref, b_ref, o_ref):
    c = c_ref[...]
    ca = c * jax.nn.sigmoid(c)
    o_ref[...] = _dot(ca.astype(BF), w_ref[...].astype(BF)) + b_ref[...]


def _ada_call(c, w, b):
    L, Dm, N = w.shape
    B = c.shape[0]
    tn = 1024
    return pl.pallas_call(
        _ada_kernel,
        grid=(L, N // tn),
        in_specs=[
            pl.BlockSpec((B, Dm), lambda l, n: (0, 0)),
            pl.BlockSpec((None, Dm, tn), lambda l, n: (l, 0, n)),
            pl.BlockSpec((None, 1, tn), lambda l, n: (l, 0, n)),
        ],
        out_specs=pl.BlockSpec((None, B, tn), lambda l, n: (l, 0, n)),
        out_shape=jax.ShapeDtypeStruct((L, B, N), F32),
        compiler_params=_params(("arbitrary", "arbitrary")),
        name="ada",
    )(c, w, b)


def _ret_in_kernel(x_ref, ada_ref, gain_ref, pos_ref, invf_ref, w_ref,
                   q_ref, k_ref, v_ref, sg_ref):
    ada = ada_ref[0]
    h = _ln_mod(x_ref[0], gain_ref[...], ada[0:1], ada[1:2]).astype(BF)
    ang = pos_ref[0].astype(F32) * invf_ref[...]
    cos = jnp.cos(ang)
    sin = jnp.sin(ang)
    half = RET_QK // 2
    qk_w = RET_HEADS * RET_QK
    for dst, base, mul in ((q_ref, 0, 1.0), (k_ref, qk_w, RET_QK ** -0.5)):
        for hh in range(RET_HEADS):
            lo = hh * RET_QK
            a = _dot(h, w_ref[:, base + lo: base + lo + RET_QK])
            x1 = a[:, :half]
            x2 = a[:, half:]
            dst[0, :, lo: lo + half] = ((x1 * cos - x2 * sin) * mul).astype(BF)
            dst[0, :, lo + half: lo + RET_QK] = ((x1 * sin + x2 * cos) * mul).astype(BF)
    v_w = RET_HEADS * RET_V
    for c in range(v_w // 512):
        a = _dot(h, w_ref[:, 2 * qk_w + c * 512: 2 * qk_w + (c + 1) * 512])
        v_ref[0, :, c * 512:(c + 1) * 512] = a.astype(BF)
    for c in range(v_w // 512):
        g = _dot(h, w_ref[:, 2 * qk_w + v_w + c * 512: 2 * qk_w + v_w + (c + 1) * 512])
        sg_ref[0, :, c * 512:(c + 1) * 512] = (g * jax.nn.sigmoid(g)).astype(BF)


def _ret_in_call(x, ada, gain, pos, invf, w):
    B, S, Dm = x.shape
    tm = 256
    qk_w = RET_HEADS * RET_QK
    v_w = RET_HEADS * RET_V
    tok = lambda n: pl.BlockSpec((1, tm, n), lambda b, s: (b, s, 0))
    return pl.pallas_call(
        _ret_in_kernel,
        grid=(B, S // tm),
        in_specs=[
            tok(Dm),
            pl.BlockSpec((1, 6, Dm), lambda b, s: (b, 0, 0)),
            pl.BlockSpec((1, Dm), lambda b, s: (0, 0)),
            tok(1),
            pl.BlockSpec((1, RET_QK // 2), lambda b, s: (0, 0)),
            _resident((None, Dm, 2 * qk_w + 2 * v_w), lambda b, s: (0, 0, 0)),
        ],
        out_specs=[tok(qk_w), tok(qk_w), tok(v_w), tok(v_w)],
        out_shape=[
            jax.ShapeDtypeStruct((B, S, qk_w), BF),
            jax.ShapeDtypeStruct((B, S, qk_w), BF),
            jax.ShapeDtypeStruct((B, S, v_w), BF),
            jax.ShapeDtypeStruct((B, S, v_w), BF),
        ],
        compiler_params=_params(("arbitrary", "arbitrary")),
        name="ret_in",
    )(x, ada, gain, pos, invf, w)


RET_STEP = 512


def _ret_kernel(chunk_decay, q_ref, k_ref, v_ref, sg_ref, gn_ref, dec_ref, qd_ref, kd_ref,
                y_ref, state_ref):
    @pl.when(pl.program_id(1) == 0)
    def _():
        state_ref[...] = jnp.zeros_like(state_ref)

    for c in range(RET_STEP // RET_CHUNK):
        rows = slice(c * RET_CHUNK, (c + 1) * RET_CHUNK)
        for hh in range(RET_HEADS):
            qs = slice(hh * RET_QK, (hh + 1) * RET_QK)
            vs = slice(hh * RET_V, (hh + 1) * RET_V)
            q = q_ref[0, rows, qs]
            k = k_ref[0, rows, qs]
            v = v_ref[0, rows, vs]
            s = _dot_nt(q, k) * dec_ref[hh]
            intra = _dot(s.astype(BF), v)
            st = state_ref[hh]
            cross = _dot(q, st.astype(BF)) * qd_ref[hh]
            kdt = (k.astype(F32) * kd_ref[hh]).T.astype(BF)
            state_ref[hh] = st * chunk_decay[hh] + _dot(kdt, v)
            y = intra + cross
            ms = jnp.mean(y * y, axis=-1, keepdims=True)
            yn = (y * lax.rsqrt(ms + EPS)) * gn_ref[:, vs]
            y_ref[0, rows, vs] = (sg_ref[0, rows, vs].astype(F32) * yn).astype(BF)


def _ret_call(q, k, v, sg, gn, dec, qd, kd, chunk_decay):
    B, S, qk_w = q.shape
    v_w = v.shape[-1]
    tok = lambda n: pl.BlockSpec((1, RET_STEP, n), lambda b, s: (b, s, 0))
    const = lambda shape: pl.BlockSpec(shape, lambda b, s: (0,) * len(shape))
    return pl.pallas_call(
        functools.partial(_ret_kernel, chunk_decay),
        grid=(B, S // RET_STEP),
        in_specs=[
            tok(qk_w), tok(qk_w), tok(v_w), tok(v_w),
            const((1, v_w)),
            const((RET_HEADS, RET_CHUNK, RET_CHUNK)),
            const((RET_HEADS, RET_CHUNK, 1)),
            const((RET_HEADS, RET_CHUNK, 1)),
        ],
        out_specs=tok(v_w),
        out_shape=jax.ShapeDtypeStruct((B, S, v_w), BF),
        scratch_shapes=[pltpu.VMEM((RET_HEADS, RET_QK, RET_V), F32)],
        compiler_params=_params(("arbitrary", "arbitrary")),
        name="retention",
    )(q, k, v, sg, gn, dec, qd, kd)


def _mlp_kernel(x_ref, y_ref, ada_ref, gain_ref, wo_ref, w1_ref, w2_ref, o_ref, hid_ref):
    ada = ada_ref[0]
    x1 = x_ref[0] + ada[2:3] * _dot(y_ref[0], wo_ref[...])
    h = _ln_mod(x1, gain_ref[...], ada[3:4], ada[4:5]).astype(BF)
    fc = 1024
    for c in range(D_FF // fc):
        a = jnp.maximum(_dot(h, w1_ref[:, c * fc:(c + 1) * fc]), 0.0)
        hid_ref[:, c * fc:(c + 1) * fc] = (a * a).astype(BF)
    o_ref[0] = x1 + ada[5:6] * _dot(hid_ref[...], w2_ref[...])


def _mlp_call(x, y, ada, gain, wo, w1, w2, layer):
    B, S, Dm = x.shape
    Ky = y.shape[-1]
    tm = 256
    tok = lambda n: pl.BlockSpec((1, tm, n), lambda b, s: (b, s, 0))
    return pl.pallas_call(
        _mlp_kernel,
        grid=(B, S // tm),
        in_specs=[
            tok(Dm), tok(Ky),
            pl.BlockSpec((1, 6, Dm), lambda b, s: (b, 0, 0)),
            pl.BlockSpec((1, Dm), lambda b, s: (0, 0)),
            _resident((None, Ky, Dm), lambda b, s: (0, 0, 0)),
            _resident((None, Dm, D_FF), lambda b, s: (layer, 0, 0)),
            _resident((None, D_FF, Dm), lambda b, s: (layer, 0, 0)),
        ],
        out_specs=tok(Dm),
        out_shape=jax.ShapeDtypeStruct((B, S, Dm), F32),
        scratch_shapes=[pltpu.VMEM((tm, D_FF), BF)],
        compiler_params=_params(("arbitrary", "arbitrary")),
        name=f"mlp{layer}",
    )(x, y, ada, gain, wo, w1, w2)


def _group_ones(n, group):
    r = lax.broadcasted_iota(jnp.int32, (n, n), 0) // group
    c = lax.broadcasted_iota(jnp.int32, (n, n), 1) // group
    return jnp.where(r == c, 1.0, 0.0).astype(BF)


def _head_rms(a, ones_blk):
    ss = _dot((a * a).astype(BF), ones_blk)
    return lax.rsqrt(ss * (1.0 / FOX_DH) + EPS)


def _split3(a):
    hi = a.astype(BF)
    r1 = a - hi.astype(F32)
    mid = r1.astype(BF)
    lo = (r1 - mid.astype(F32)).astype(BF)
    return hi, mid, lo


def _log_sigmoid(z):
    return jnp.minimum(z, 0.0) - jnp.log(1.0 + jnp.exp(-jnp.abs(z)))


KV_TM = 256


def _kv_kernel(x_ref, ada_ref, gain_ref, w_ref, wf_ref, wft_ref, fb_row_ref, fb_col_ref, kg_ref,
               k_ref, v_ref, fq_ref, fkt_ref, carry_row, carry_col):
    @pl.when(pl.program_id(1) == 0)
    def _():
        carry_row[...] = jnp.zeros_like(carry_row)
        carry_col[...] = jnp.zeros_like(carry_col)

    tm = KV_TM
    ada = ada_ref[0]
    h = _ln_mod(x_ref[0], gain_ref[...], ada[0:1], ada[1:2]).astype(BF)
    ones_blk = _group_ones(MXU_DIM, FOX_DH)
    for sl in range(D_MODEL // MXU_DIM):
        cs = slice(sl * MXU_DIM, (sl + 1) * MXU_DIM)
        a = _dot(h, w_ref[:, cs])
        k_ref[0, :, cs] = (a * _head_rms(a, ones_blk) * kg_ref[:, cs]).astype(BF)
    for c in range(2):
        cs = slice(c * 512, (c + 1) * 512)
        v_ref[0, :, cs] = _dot(h, w_ref[:, D_MODEL + c * 512: D_MODEL + (c + 1) * 512]).astype(BF)

    ri = lax.broadcasted_iota(jnp.int32, (tm, tm), 0)
    ci = lax.broadcasted_iota(jnp.int32, (tm, tm), 1)
    lower = jnp.where(ci <= ri, 1.0, 0.0).astype(BF)
    upper = jnp.where(ri <= ci, 1.0, 0.0).astype(BF)
    lf = _log_sigmoid(_dot(h, wf_ref[...]) + fb_row_ref[...])
    hi, mid, lo = _split3(lf)
    cum = (_dot(lower, hi) + _dot(lower, mid)) + _dot(lower, lo) + carry_row[...]
    fq_ref[0] = cum
    carry_row[...] = cum[tm - 1:tm, :]

    lft = _log_sigmoid(_dot_nt(wft_ref[...], h) + fb_col_ref[...])
    hi, mid, lo = _split3(lft)
    cumt = (_dot(hi, upper) + _dot(mid, upper)) + _dot(lo, upper) + carry_col[:, 0:1]
    fkt_ref[0] = cumt
    carry_col[...] = jnp.broadcast_to(cumt[:, tm - 1:tm], carry_col.shape)


def _kv_call(x, ada, gain, w, wf, wft, fb_row, fb_col, kg):
    B, S, Dm = x.shape
    tm = KV_TM
    tok = lambda n: pl.BlockSpec((1, tm, n), lambda b, s: (b, s, 0))
    const = lambda shape: pl.BlockSpec(shape, lambda b, s: (0,) * len(shape))
    return pl.pallas_call(
        _kv_kernel,
        grid=(B, S // tm),
        in_specs=[
            tok(Dm),
            pl.BlockSpec((1, 2, Dm), lambda b, s: (b, 0, 0)),
            const((1, Dm)),
            _resident((Dm, 2 * Dm), lambda b, s: (0, 0)),
            const((Dm, LANES)),
            const((FOX_HEADS, Dm)),
            const((1, LANES)),
            const((FOX_HEADS, 1)),
            const((1, Dm)),
        ],
        out_specs=[
            tok(Dm), tok(Dm), tok(LANES),
            pl.BlockSpec((1, FOX_HEADS, tm), lambda b, s: (b, 0, s)),
        ],
        out_shape=[
            jax.ShapeDtypeStruct((B, S, Dm), BF),
            jax.ShapeDtypeStruct((B, S, Dm), BF),
            jax.ShapeDtypeStruct((B, S, LANES), F32),
            jax.ShapeDtypeStruct((B, FOX_HEADS, S), F32),
        ],
        scratch_shapes=[pltpu.VMEM((1, LANES), F32), pltpu.VMEM((FOX_HEADS, LANES), F32)],
        compiler_params=_params(("arbitrary", "arbitrary")),
        name="shared_kv",
    )(x, ada, gain, w, wf, wft, fb_row, fb_col, kg)


def _fox_in_kernel(x_ref, ada_ref, gain_ref, w_ref, qg_ref, q_ref, sog_ref):
    ada = ada_ref[0]
    h = _ln_mod(x_ref[0], gain_ref[...], ada[0:1], ada[1:2]).astype(BF)
    ones_blk = _group_ones(MXU_DIM, FOX_DH)
    for sl in range(D_MODEL // MXU_DIM):
        cs = slice(sl * MXU_DIM, (sl + 1) * MXU_DIM)
        a = _dot(h, w_ref[:, cs])
        q_ref[0, :, cs] = (a * _head_rms(a, ones_blk) * qg_ref[:, cs] * (FOX_DH ** -0.5)).astype(BF)
    for c in range(2):
        og = _dot(h, w_ref[:, D_MODEL + c * 512: D_MODEL + (c + 1) * 512])
        sog_ref[0, :, c * 512:(c + 1) * 512] = jax.nn.sigmoid(og).astype(BF)


def _fox_in_call(x, ada, gain, w, qg):
    B, S, Dm = x.shape
    tm = 256
    tok = lambda n: pl.BlockSpec((1, tm, n), lambda b, s: (b, s, 0))
    return pl.pallas_call(
        _fox_in_kernel,
        grid=(B, S // tm),
        in_specs=[
            tok(Dm),
            pl.BlockSpec((1, 6, Dm), lambda b, s: (b, 0, 0)),
            pl.BlockSpec((1, Dm), lambda b, s: (0, 0)),
            _resident((None, Dm, 2 * Dm), lambda b, s: (0, 0, 0)),
            pl.BlockSpec((1, Dm), lambda b, s: (0, 0)),
        ],
        out_specs=[tok(Dm), tok(Dm)],
        out_shape=[jax.ShapeDtypeStruct((B, S, Dm), BF), jax.ShapeDtypeStruct((B, S, Dm), BF)],
        compiler_params=_params(("arbitrary", "arbitrary")),
        name="fox_in",
    )(x, ada, gain, w, qg)


ATT_T = 256


def _attn_kernel(q_ref, k_ref, v_ref, fq_ref, fkt_ref, sog_ref, y_ref):
    t = ATT_T
    qi = pl.program_id(1)
    lane = lax.broadcasted_iota(jnp.int32, (1, LANES), 1)
    mask_a = jnp.where(lane < FOX_DH, 1.0, 0.0)
    mask_b = 1.0 - mask_a
    causal = (lax.broadcasted_iota(jnp.int32, (t, t), 1)
              <= lax.broadcasted_iota(jnp.int32, (t, t), 0))
    fq = fq_ref[0]

    for p in range(FOX_HEADS // 2):
        cs = slice(p * LANES, (p + 1) * LANES)
        qf = q_ref[0, :, cs].astype(F32)
        qs = ((qf * mask_a).astype(BF), (qf * mask_b).astype(BF))
        fqs = (fq[:, 2 * p:2 * p + 1], fq[:, 2 * p + 1:2 * p + 2])

        def block(j, carry, masked):
            kb = k_ref[0, pl.ds(pl.multiple_of(j * t, t), t), cs]
            vb = v_ref[0, pl.ds(pl.multiple_of(j * t, t), t), cs]
            out = []
            for hh in range(2):
                m, l, acc = carry[hh]
                fk = fkt_ref[0, 2 * p + hh, pl.ds(j, 1), :]
                s = _dot_nt(qs[hh], kb) + fqs[hh] - fk
                if masked:
                    s = jnp.where(causal, s, -jnp.inf)
                m_new = jnp.maximum(m, jnp.max(s, axis=-1, keepdims=True))
                pr = jnp.exp(s - m_new)
                alpha = jnp.exp(m - m_new)
                l_new = alpha * l + jnp.sum(pr, axis=-1, keepdims=True)
                acc_new = alpha * acc + _dot(pr.astype(BF), vb)
                out.append((m_new, l_new, acc_new))
            return tuple(out)

        one = (jnp.full((t, 1), -jnp.inf, F32), jnp.zeros((t, 1), F32), jnp.zeros((t, LANES), F32))
        carry = lax.fori_loop(0, qi, lambda j, c: block(j, c, False), (one, one))
        (_, l0, a0), (_, l1, a1) = block(qi, carry, True)
        out = (a0 / l0) * mask_a + (a1 / l1) * mask_b
        y_ref[0, :, cs] = (sog_ref[0, :, cs].astype(F32) * out).astype(BF)


def _attn_call(q, k, v, fq, fkt, sog):
    B, S, Dm = q.shape
    t = ATT_T
    tok = lambda n: pl.BlockSpec((1, t, n), lambda b, s: (b, s, 0))
    return pl.pallas_call(
        _attn_kernel,
        grid=(B, S // t),
        in_specs=[
            tok(Dm),
            pl.BlockSpec((1, S, Dm), lambda b, s: (b, 0, 0)),
            pl.BlockSpec((1, S, Dm), lambda b, s: (b, 0, 0)),
            tok(LANES),
            pl.BlockSpec((1, FOX_HEADS, S // t, t), lambda b, s: (b, 0, 0, 0)),
            tok(Dm),
        ],
        out_specs=tok(Dm),
        out_shape=jax.ShapeDtypeStruct((B, S, Dm), BF),
        compiler_params=_params(("arbitrary", "arbitrary")),
        name="fox_attn",
    )(q, k, v, fq, fkt, sog)


def _retention_constants():
    gam = 1.0 - np.power(2.0, -5.0 - np.arange(RET_HEADS, dtype=np.float64))
    log_g = np.log(gam)
    idx = np.arange(RET_CHUNK, dtype=np.float64)
    rel = idx[:, None] - idx[None, :]
    dec = np.where(rel >= 0, np.exp(log_g[:, None, None] * np.maximum(rel, 0.0)), 0.0)
    qd = np.exp(log_g[:, None] * (idx + 1.0))[:, :, None]
    kd = np.exp(log_g[:, None] * (RET_CHUNK - 1.0 - idx))[:, :, None]
    chunk_decay = tuple(float(v) for v in np.exp(log_g * RET_CHUNK))
    return (jnp.asarray(dec, F32), jnp.asarray(qd, F32), jnp.asarray(kd, F32), chunk_decay)


def kernel(x, c, positions, norm_mix_gain, norm_mlp_gain, w_ada, b_ada, w_mlp_in, w_mlp_out,
           ret_w_in, ret_norm_gain, ret_w_out, kv_norm_gain, kv_w_ada, kv_b_ada, kv_w,
           forget_bias, k_norm_gain, fox_w_in, q_norm_gain, fox_w_out):
    B, S, Dm = x.shape

    ada = _ada_call(c, w_ada, b_ada[:, None, :])
    ada = ada.reshape(2, B, 6, Dm)
    ada_kv = _ada_call(c, kv_w_ada[None], kv_b_ada[None, None, :])
    ada_kv = ada_kv.reshape(B, 2, Dm)

    w1 = w_mlp_in.astype(BF)
    w2 = w_mlp_out.astype(BF)
    rwi = ret_w_in.astype(BF)
    rwo = ret_w_out.astype(BF)
    kvw = kv_w[:, :2 * Dm].astype(BF)
    wf = kv_w[:, 2 * Dm:]
    wf_pad = jnp.pad(wf, ((0, 0), (0, LANES - FOX_HEADS))).astype(BF)
    wft = wf.T.astype(BF)
    fwi = fox_w_in.astype(BF)
    fwo = fox_w_out.astype(BF)

    half = RET_QK // 2
    invf = (ROPE_BASE ** (-jnp.arange(half, dtype=F32) / half))[None, :]
    q, k, v, sg = _ret_in_call(x, ada[0], norm_mix_gain[0:1], positions[:, :, None], invf, rwi)
    dec, qd, kd, chunk_decay = _retention_constants()
    y = _ret_call(q, k, v, sg, ret_norm_gain[0].reshape(1, RET_HEADS * RET_V), dec, qd, kd,
                  chunk_decay)
    x = _mlp_call(x, y, ada[0], norm_mlp_gain[0:1], rwo, w1, w2, 0)

    fb_row = jnp.pad(forget_bias, (0, LANES - FOX_HEADS))[None, :]
    fb_col = forget_bias[:, None]
    kg = jnp.tile(k_norm_gain, FOX_HEADS)[None, :]
    ks, vs, fq, fkt = _kv_call(x, ada_kv, kv_norm_gain[None, :], kvw, wf_pad, wft, fb_row, fb_col, kg)

    qg = jnp.tile(q_norm_gain[0], FOX_HEADS)[None, :]
    qf, sog = _fox_in_call(x, ada[1], norm_mix_gain[1:2], fwi, qg)
    y = _attn_call(qf, ks, vs, fq, fkt.reshape(B, FOX_HEADS, S // ATT_T, ATT_T), sog)
    x = _mlp_call(x, y, ada[1], norm_mlp_gain[1:2], fwo, w1, w2, 1)
    return x
```

```python
import functools
import math

import jax
import jax.numpy as jnp
import numpy as np
from jax import lax
from jax.experimental import pallas as pl
from jax.experimental.pallas import tpu as pltpu

F32 = jnp.float32
BF = jnp.bfloat16

D_MODEL = 1024
RET_HEADS = 4
RET_QK = 256
RET_V = 512
RET_CHUNK = 128
FOX_HEADS = 16
FOX_DH = 64
D_FF = 4096
EPS = 1e-6
ROPE_BASE = 10000.0
LOG2E = math.log2(math.e)
NEG_BIG = -1e30
LANES = 128
MXU_DIM = 256
VMEM_LIMIT = 48 * 1024 * 1024

BIAS_ONE0 = 3 * FOX_HEADS


def _params(sem):
    return pltpu.CompilerParams(dimension_semantics=sem, vmem_limit_bytes=VMEM_LIMIT)


def _resident(shape, index_map):
    return pl.BlockSpec(shape, index_map, pipeline_mode=pl.Buffered(1))


def _ln_mod(x, gain, shift, scale):
    ms = jnp.mean(x * x, axis=-1, keepdims=True)
    y = x * lax.rsqrt(ms + EPS)
    return (y * gain) * (1.0 + scale) + shift


def _dot(a, b):
    return jnp.dot(a, b, preferred_element_type=F32)


def _dot_nt(a, b):
    return lax.dot_general(a, b, (((1,), (1,)), ((), ())), preferred_element_type=F32)


def _split3(a):
    hi = a.astype(BF).astype(F32)
    r1 = a - hi
    mid = r1.astype(BF).astype(F32)
    lo = (r1 - mid).astype(BF).astype(F32)
    return hi, mid, lo


def _ada_kernel(c_ref, w_ref, b_ref, o_ref):
    c = c_ref[...]
    ca = c * jax.nn.sigmoid(c)
    o_ref[...] = _dot(ca.astype(BF), w_ref[...].astype(BF)) + b_ref[...]


def _ada_call(c, w, b):
    L, Dm, N = w.shape
    B = c.shape[0]
    tn = 1024
    return pl.pallas_call(
        _ada_kernel,
        grid=(L, N // tn),
        in_specs=[
            pl.BlockSpec((B, Dm), lambda l, n: (0, 0)),
            pl.BlockSpec((None, Dm, tn), lambda l, n: (l, 0, n)),
            pl.BlockSpec((None, 1, tn), lambda l, n: (l, 0, n)),
        ],
        out_specs=pl.BlockSpec((None, B, tn), lambda l, n: (l, 0, n)),
        out_shape=jax.ShapeDtypeStruct((L, B, N), F32),
        compiler_params=_params(("arbitrary", "arbitrary")),
        name="ada",
    )(c, w, b)


def _ret_in_kernel(x_ref, ada_ref, gain_ref, pos_ref, invf_ref, w_ref,
                   q_ref, k_ref, v_ref, sg_ref):
    ada = ada_ref[0]
    h = _ln_mod(x_ref[0], gain_ref[...], ada[0:1], ada[1:2]).astype(BF)
    ang = pos_ref[0].astype(F32) * invf_ref[...]
    cos = jnp.cos(ang)
    sin = jnp.sin(ang)
    half = RET_QK // 2
    qk_w = RET_HEADS * RET_QK
    for dst, base, mul in ((q_ref, 0, 1.0), (k_ref, qk_w, RET_QK ** -0.5)):
        for hh in range(RET_HEADS):
            lo = hh * RET_QK
            a = _dot(h, w_ref[:, base + lo: base + lo + RET_QK])
            x1 = a[:, :half]
            x2 = a[:, half:]
            dst[0, :, lo: lo + half] = ((x1 * cos - x2 * sin) * mul).astype(BF)
            dst[0, :, lo + half: lo + RET_QK] = ((x1 * sin + x2 * cos) * mul).astype(BF)
    v_w = RET_HEADS * RET_V
    for c in range(v_w // 512):
        a = _dot(h, w_ref[:, 2 * qk_w + c * 512: 2 * qk_w + (c + 1) * 512])
        v_ref[0, :, c * 512:(c + 1) * 512] = a.astype(BF)
    for c in range(v_w // 512):
        g = _dot(h, w_ref[:, 2 * qk_w + v_w + c * 512: 2 * qk_w + v_w + (c + 1) * 512])
        sg_ref[0, :, c * 512:(c + 1) * 512] = (g * jax.nn.sigmoid(g)).astype(BF)


def _ret_in_call(x, ada, gain, pos, invf, w):
    B, S, Dm = x.shape
    tm = 256
    qk_w = RET_HEADS * RET_QK
    v_w = RET_HEADS * RET_V
    tok = lambda n: pl.BlockSpec((1, tm, n), lambda b, s: (b, s, 0))
    return pl.pallas_call(
        _ret_in_kernel,
        grid=(B, S // tm),
        in_specs=[
            tok(Dm),
            pl.BlockSpec((1, 6, Dm), lambda b, s: (b, 0, 0)),
            pl.BlockSpec((1, Dm), lambda b, s: (0, 0)),
            tok(1),
            pl.BlockSpec((1, RET_QK // 2), lambda b, s: (0, 0)),
            _resident((None, Dm, 2 * qk_w + 2 * v_w), lambda b, s: (0, 0, 0)),
        ],
        out_specs=[tok(qk_w), tok(qk_w), tok(v_w), tok(v_w)],
        out_shape=[
            jax.ShapeDtypeStruct((B, S, qk_w), BF),
            jax.ShapeDtypeStruct((B, S, qk_w), BF),
            jax.ShapeDtypeStruct((B, S, v_w), BF),
            jax.ShapeDtypeStruct((B, S, v_w), BF),
        ],
        compiler_params=_params(("arbitrary", "arbitrary")),
        name="ret_in",
    )(x, ada, gain, pos, invf, w)


RET_STEP = 512


def _ret_kernel(chunk_decay, q_ref, k_ref, v_ref, sg_ref, gn_ref, dec_ref, qd_ref, kd_ref,
                y_ref, state_ref):
    @pl.when(pl.program_id(1) == 0)
    def _():
        state_ref[...] = jnp.zeros_like(state_ref)

    for c in range(RET_STEP // RET_CHUNK):
        rows = slice(c * RET_CHUNK, (c + 1) * RET_CHUNK)
        for hh in range(RET_HEADS):
            qs = slice(hh * RET_QK, (hh + 1) * RET_QK)
            vs = slice(hh * RET_V, (hh + 1) * RET_V)
            q = q_ref[0, rows, qs]
            k = k_ref[0, rows, qs]
            v = v_ref[0, rows, vs]
            s = _dot_nt(q, k) * dec_ref[hh]
            intra = _dot(s.astype(BF), v)
            st = state_ref[hh]
            cross = _dot(q, st.astype(BF)) * qd_ref[hh]
            kdt = (k.astype(F32) * kd_ref[hh]).T.astype(BF)
            state_ref[hh] = st * chunk_decay[hh] + _dot(kdt, v)
            y = intra + cross
            ms = jnp.mean(y * y, axis=-1, keepdims=True)
            yn = (y * lax.rsqrt(ms + EPS)) * gn_ref[:, vs]
            y_ref[0, rows, vs] = (sg_ref[0, rows, vs].astype(F32) * yn).astype(BF)


def _ret_call(q, k, v, sg, gn, dec, qd, kd, chunk_decay):
    B, S, qk_w = q.shape
    v_w = v.shape[-1]
    tok = lambda n: pl.BlockSpec((1, RET_STEP, n), lambda b, s: (b, s, 0))
    const = lambda shape: pl.BlockSpec(shape, lambda b, s: (0,) * len(shape))
    return pl.pallas_call(
        functools.partial(_ret_kernel, chunk_decay),
        grid=(B, S // RET_STEP),
        in_specs=[
            tok(qk_w), tok(qk_w), tok(v_w), tok(v_w),
            const((1, v_w)),
            const((RET_HEADS, RET_CHUNK, RET_CHUNK)),
            const((RET_HEADS, RET_CHUNK, 1)),
            const((RET_HEADS, RET_CHUNK, 1)),
        ],
        out_specs=tok(v_w),
        out_shape=jax.ShapeDtypeStruct((B, S, v_w), BF),
        scratch_shapes=[pltpu.VMEM((RET_HEADS, RET_QK, RET_V), F32)],
        compiler_params=_params(("arbitrary", "arbitrary")),
        name="retention",
    )(q, k, v, sg, gn, dec, qd, kd)


def _mlp_kernel(x_ref, y_ref, ada_ref, gain_ref, wo_ref, w1_ref, w2_ref, o_ref, hid_ref):
    ada = ada_ref[0]
    x1 = x_ref[0] + ada[2:3] * _dot(y_ref[0], wo_ref[...])
    h = _ln_mod(x1, gain_ref[...], ada[3:4], ada[4:5]).astype(BF)
    fc = 1024
    for c in range(D_FF // fc):
        a = jnp.maximum(_dot(h, w1_ref[:, c * fc:(c + 1) * fc]), 0.0)
        hid_ref[:, c * fc:(c + 1) * fc] = (a * a).astype(BF)
    o_ref[0] = x1 + ada[5:6] * _dot(hid_ref[...], w2_ref[...])


def _mlp_call(x, y, ada, gain, wo, w1, w2, layer):
    B, S, Dm = x.shape
    Ky = y.shape[-1]
    tm = 256
    tok = lambda n: pl.BlockSpec((1, tm, n), lambda b, s: (b, s, 0))
    return pl.pallas_call(
        _mlp_kernel,
        grid=(B, S // tm),
        in_specs=[
            tok(Dm), tok(Ky),
            pl.BlockSpec((1, 6, Dm), lambda b, s: (b, 0, 0)),
            pl.BlockSpec((1, Dm), lambda b, s: (0, 0)),
            _resident((None, Ky, Dm), lambda b, s: (0, 0, 0)),
            _resident((None, Dm, D_FF), lambda b, s: (layer, 0, 0)),
            _resident((None, D_FF, Dm), lambda b, s: (layer, 0, 0)),
        ],
        out_specs=tok(Dm),
        out_shape=jax.ShapeDtypeStruct((B, S, Dm), F32),
        scratch_shapes=[pltpu.VMEM((tm, D_FF), BF)],
        compiler_params=_params(("arbitrary", "arbitrary")),
        name=f"mlp{layer}",
    )(x, y, ada, gain, wo, w1, w2)


def _group_ones(n, group):
    r = lax.broadcasted_iota(jnp.int32, (n, n), 0) // group
    c = lax.broadcasted_iota(jnp.int32, (n, n), 1) // group
    return jnp.where(r == c, 1.0, 0.0).astype(BF)


def _head_rms(a, ones_blk):
    ss = _dot((a * a).astype(BF), ones_blk)
    return lax.rsqrt(ss * (1.0 / FOX_DH) + EPS)


def _log_sigmoid(z):
    return jnp.minimum(z, 0.0) - jnp.log(1.0 + jnp.exp(-jnp.abs(z)))


ATT_T = 256
VT_PAD = 16
VT_ROWS = FOX_DH + VT_PAD


def _kv_kernel(x_ref, ada_ref, gain_ref, w_ref, wf3_ref, wft_ref, fb3_ref, fb_col_ref, kg_ref,
               k_ref, kb_ref, vt_ref, f2t_ref, carry_row, carry_col):
    @pl.when(pl.program_id(1) == 0)
    def _():
        carry_row[...] = jnp.zeros_like(carry_row)
        carry_col[...] = jnp.zeros_like(carry_col)

    tm = ATT_T
    ada = ada_ref[0]
    h = _ln_mod(x_ref[0], gain_ref[...], ada[0:1], ada[1:2]).astype(BF)
    ones_blk = _group_ones(MXU_DIM, FOX_DH)
    for sl in range(D_MODEL // MXU_DIM):
        cs = slice(sl * MXU_DIM, (sl + 1) * MXU_DIM)
        a = _dot(h, w_ref[:, cs])
        k_ref[0, :, cs] = (a * _head_rms(a, ones_blk) * kg_ref[:, cs]).astype(BF)
    pad_rows = jnp.where(lax.broadcasted_iota(jnp.int32, (VT_PAD, tm), 0) == 0, 1.0, 0.0).astype(BF)
    for c in range(2):
        vt = _dot(h, w_ref[:, D_MODEL + c * 512: D_MODEL + (c + 1) * 512]).T.astype(BF)
        for i in range(512 // FOX_DH):
            r0 = (c * (512 // FOX_DH) + i) * VT_ROWS
            vt_ref[0, 0, r0:r0 + FOX_DH, :] = vt[i * FOX_DH:(i + 1) * FOX_DH]
            vt_ref[0, 0, r0 + FOX_DH:r0 + VT_ROWS, :] = pad_rows

    ri = lax.broadcasted_iota(jnp.int32, (tm, tm), 0)
    ci = lax.broadcasted_iota(jnp.int32, (tm, tm), 1)
    lower = jnp.where(ci <= ri, 1.0, 0.0).astype(BF)
    upper = jnp.where(ri <= ci, 1.0, 0.0).astype(BF)
    hi, mid, lo = _split3(_log_sigmoid(_dot(h, wf3_ref[...]) + fb3_ref[...]))
    cum = ((_dot(lower, hi.astype(BF)) + _dot(lower, mid.astype(BF)))
           + _dot(lower, lo.astype(BF)) + carry_row[...])
    carry_row[...] = cum[tm - 1:tm, :]
    hi, mid, lo = _split3(cum * LOG2E)
    lane = lax.broadcasted_iota(jnp.int32, (tm, LANES), 1)
    ones_lane = jnp.where(lane < BIAS_ONE0 + 3, 1.0, 0.0)
    kb = jnp.where(lane < FOX_HEADS, -hi,
                   jnp.where(lane < 2 * FOX_HEADS, -mid,
                             jnp.where(lane < BIAS_ONE0, -lo, ones_lane)))
    kb_ref[0] = kb.astype(BF)

    hi, mid, lo = _split3(_log_sigmoid(_dot_nt(wft_ref[...], h) + fb_col_ref[...]))
    cumt = ((_dot(hi.astype(BF), upper) + _dot(mid.astype(BF), upper))
            + _dot(lo.astype(BF), upper) + carry_col[:, 0:1])
    carry_col[...] = jnp.broadcast_to(cumt[:, tm - 1:tm], carry_col.shape)
    f2t_ref[0] = cumt * LOG2E


def _kv_call(x, ada, gain, w, wf3, wft, fb3, fb_col, kg):
    B, S, Dm = x.shape
    tm = ATT_T
    tok = lambda n: pl.BlockSpec((1, tm, n), lambda b, s: (b, s, 0))
    const = lambda shape: pl.BlockSpec(shape, lambda b, s: (0,) * len(shape))
    return pl.pallas_call(
        _kv_kernel,
        grid=(B, S // tm),
        in_specs=[
            tok(Dm),
            pl.BlockSpec((1, 2, Dm), lambda b, s: (b, 0, 0)),
            const((1, Dm)),
            _resident((Dm, 2 * Dm), lambda b, s: (0, 0)),
            const((Dm, LANES)),
            const((FOX_HEADS, Dm)),
            const((1, LANES)),
            const((FOX_HEADS, 1)),
            const((1, Dm)),
        ],
        out_specs=[
            tok(Dm), tok(LANES),
            pl.BlockSpec((1, 1, FOX_HEADS * VT_ROWS, tm), lambda b, s: (b, s, 0, 0)),
            pl.BlockSpec((1, FOX_HEADS, tm), lambda b, s: (b, 0, s)),
        ],
        out_shape=[
            jax.ShapeDtypeStruct((B, S, Dm), BF),
            jax.ShapeDtypeStruct((B, S, LANES), BF),
            jax.ShapeDtypeStruct((B, S // tm, FOX_HEADS * VT_ROWS, tm), BF),
            jax.ShapeDtypeStruct((B, FOX_HEADS, S), F32),
        ],
        scratch_shapes=[pltpu.VMEM((1, LANES), F32), pltpu.VMEM((FOX_HEADS, LANES), F32)],
        compiler_params=_params(("arbitrary", "arbitrary")),
        name="shared_kv",
    )(x, ada, gain, w, wf3, wft, fb3, fb_col, kg)


def _fox_in_kernel(x_ref, ada_ref, gain_ref, w_ref, qg_ref, q_ref, sog_ref):
    ada = ada_ref[0]
    h = _ln_mod(x_ref[0], gain_ref[...], ada[0:1], ada[1:2]).astype(BF)
    ones_blk = _group_ones(MXU_DIM, FOX_DH)
    qscale = (FOX_DH ** -0.5) * LOG2E
    for sl in range(D_MODEL // MXU_DIM):
        cs = slice(sl * MXU_DIM, (sl + 1) * MXU_DIM)
        a = _dot(h, w_ref[:, cs])
        q_ref[0, :, cs] = (a * _head_rms(a, ones_blk) * qg_ref[:, cs] * qscale).astype(BF)
    for c in range(2):
        og = _dot(h, w_ref[:, D_MODEL + c * 512: D_MODEL + (c + 1) * 512])
        sog_ref[0, :, c * 512:(c + 1) * 512] = jax.nn.sigmoid(og).astype(BF)


def _fox_in_call(x, ada, gain, w, qg):
    B, S, Dm = x.shape
    tm = 256
    tok = lambda n: pl.BlockSpec((1, tm, n), lambda b, s: (b, s, 0))
    return pl.pallas_call(
        _fox_in_kernel,
        grid=(B, S // tm),
        in_specs=[
            tok(Dm),
            pl.BlockSpec((1, 6, Dm), lambda b, s: (b, 0, 0)),
            pl.BlockSpec((1, Dm), lambda b, s: (0, 0)),
            _resident((None, Dm, 2 * Dm), lambda b, s: (0, 0, 0)),
            pl.BlockSpec((1, Dm), lambda b, s: (0, 0)),
        ],
        out_specs=[tok(Dm), tok(Dm)],
        out_shape=[jax.ShapeDtypeStruct((B, S, Dm), BF), jax.ShapeDtypeStruct((B, S, Dm), BF)],
        compiler_params=_params(("arbitrary", "arbitrary")),
        name="fox_in",
    )(x, ada, gain, w, qg)


ATT_LOOKAHEAD = 8


def _attn_block(j, masked, k_ref, kb_ref, vt_ref, qa_ref, m_ref, acc_ref):
    t = ATT_T
    off = pl.multiple_of(j * t, t)
    kbias = kb_ref[0, pl.ds(off, t), :]
    if masked:
        causal = (lax.broadcasted_iota(jnp.int32, (t, t), 0)
                  <= lax.broadcasted_iota(jnp.int32, (t, t), 1))
    kcat = {}
    scores = {}
    for g in range(FOX_HEADS + ATT_LOOKAHEAD):
        if g < FOX_HEADS:
            p = g // 2
            if p not in kcat:
                kcat[p] = jnp.concatenate(
                    [k_ref[0, pl.ds(off, t), p * LANES:(p + 1) * LANES], kbias], axis=1)
            scores[g] = _dot(kcat[p], qa_ref[g])
        hd = g - ATT_LOOKAHEAD
        if hd >= 0:
            s = scores.pop(hd)
            if masked:
                s = jnp.where(causal, s, NEG_BIG)
            m = m_ref[hd:hd + 1, :]
            m_new = jnp.maximum(m, jnp.max(s, axis=0, keepdims=True))
            pr = jnp.exp2(s - m_new)
            alpha = jnp.exp2(m - m_new)
            m_ref[hd:hd + 1, :] = m_new
            vth = vt_ref[0, j, hd * VT_ROWS:(hd + 1) * VT_ROWS, :]
            acc_ref[hd] = alpha * acc_ref[hd] + _dot(vth, pr.astype(BF))


def _attn_kernel(q_ref, k_ref, kb_ref, vt_ref, f2t_ref, sog_ref, y_ref,
                 qa_ref, m_ref, acc_ref):
    t = ATT_T
    qi = pl.program_id(1)
    row = lax.broadcasted_iota(jnp.int32, (LANES, t), 0)
    zeros_half = jnp.zeros((FOX_DH, t), F32)

    for p in range(FOX_HEADS // 2):
        cs = slice(p * LANES, (p + 1) * LANES)
        qt = q_ref[0, :, cs].astype(F32).T
        for hh in range(2):
            hd = 2 * p + hh
            fhi, fmid, flo = _split3(f2t_ref[0, hd:hd + 1, :])
            bias = (jnp.where(row == hd, 1.0, 0.0)
                    + jnp.where(row == FOX_HEADS + hd, 1.0, 0.0)
                    + jnp.where(row == 2 * FOX_HEADS + hd, 1.0, 0.0)
                    + jnp.where(row == BIAS_ONE0, fhi, 0.0)
                    + jnp.where(row == BIAS_ONE0 + 1, fmid, 0.0)
                    + jnp.where(row == BIAS_ONE0 + 2, flo, 0.0))
            if hh == 0:
                top = jnp.concatenate([qt[:FOX_DH], zeros_half], axis=0)
            else:
                top = jnp.concatenate([zeros_half, qt[FOX_DH:]], axis=0)
            qa_ref[hd] = jnp.concatenate([top, bias], axis=0).astype(BF)

    m_ref[...] = jnp.full(m_ref.shape, NEG_BIG, F32)
    acc_ref[...] = jnp.zeros_like(acc_ref)

    refs = (k_ref, kb_ref, vt_ref, qa_ref, m_ref, acc_ref)

    def body(j, carry):
        _attn_block(j, False, *refs)
        return carry

    lax.fori_loop(0, qi, body, 0)
    _attn_block(qi, True, *refs)

    for p in range(FOX_HEADS // 2):
        cs = slice(p * LANES, (p + 1) * LANES)
        halves = []
        for hd in (2 * p, 2 * p + 1):
            num = acc_ref[hd, :FOX_DH, :]
            den = acc_ref[hd, FOX_DH:FOX_DH + 1, :]
            halves.append(num * (1.0 / den))
        ot = jnp.concatenate(halves, axis=0)
        y_ref[0, :, cs] = (sog_ref[0, :, cs].astype(F32) * ot.T).astype(BF)


def _attn_call(q, k, kb, vt, f2t, sog):
    B, S, Dm = q.shape
    t = ATT_T
    tok = lambda n: pl.BlockSpec((1, t, n), lambda b, s: (b, s, 0))
    return pl.pallas_call(
        _attn_kernel,
        grid=(B, S // t),
        in_specs=[
            tok(Dm),
            pl.BlockSpec((1, S, Dm), lambda b, s: (b, 0, 0)),
            pl.BlockSpec((1, S, LANES), lambda b, s: (b, 0, 0)),
            pl.BlockSpec((1, S // t, FOX_HEADS * VT_ROWS, t), lambda b, s: (b, 0, 0, 0)),
            pl.BlockSpec((1, FOX_HEADS, t), lambda b, s: (b, 0, s)),
            tok(Dm),
        ],
        out_specs=tok(Dm),
        out_shape=jax.ShapeDtypeStruct((B, S, Dm), BF),
        scratch_shapes=[
            pltpu.VMEM((FOX_HEADS, 2 * LANES, t), BF),
            pltpu.VMEM((FOX_HEADS, t), F32),
            pltpu.VMEM((FOX_HEADS, VT_ROWS, t), F32),
        ],
        compiler_params=_params(("arbitrary", "arbitrary")),
        name="fox_attn",
    )(q, k, kb, vt, f2t, sog)


def _retention_constants():
    gam = 1.0 - np.power(2.0, -5.0 - np.arange(RET_HEADS, dtype=np.float64))
    log_g = np.log(gam)
    idx = np.arange(RET_CHUNK, dtype=np.float64)
    rel = idx[:, None] - idx[None, :]
    dec = np.where(rel >= 0, np.exp(log_g[:, None, None] * np.maximum(rel, 0.0)), 0.0)
    qd = np.exp(log_g[:, None] * (idx + 1.0))[:, :, None]
    kd = np.exp(log_g[:, None] * (RET_CHUNK - 1.0 - idx))[:, :, None]
    chunk_decay = tuple(float(v) for v in np.exp(log_g * RET_CHUNK))
    return (jnp.asarray(dec, F32), jnp.asarray(qd, F32), jnp.asarray(kd, F32), chunk_decay)


def kernel(x, c, positions, norm_mix_gain, norm_mlp_gain, w_ada, b_ada, w_mlp_in, w_mlp_out,
           ret_w_in, ret_norm_gain, ret_w_out, kv_norm_gain, kv_w_ada, kv_b_ada, kv_w,
           forget_bias, k_norm_gain, fox_w_in, q_norm_gain, fox_w_out):
    B, S, Dm = x.shape

    ada = _ada_call(c, w_ada, b_ada[:, None, :])
    ada = ada.reshape(2, B, 6, Dm)
    ada_kv = _ada_call(c, kv_w_ada[None], kv_b_ada[None, None, :])
    ada_kv = ada_kv.reshape(B, 2, Dm)

    w1 = w_mlp_in.astype(BF)
    w2 = w_mlp_out.astype(BF)
    rwi = ret_w_in.astype(BF)
    rwo = ret_w_out.astype(BF)
    kvw = kv_w[:, :2 * Dm].astype(BF)
    wf = kv_w[:, 2 * Dm:]
    wf3 = jnp.pad(jnp.tile(wf, (1, 3)), ((0, 0), (0, LANES - BIAS_ONE0))).astype(BF)
    wft = wf.T.astype(BF)
    fwi = fox_w_in.astype(BF)
    fwo = fox_w_out.astype(BF)

    half = RET_QK // 2
    invf = (ROPE_BASE ** (-jnp.arange(half, dtype=F32) / half))[None, :]
    q, k, v, sg = _ret_in_call(x, ada[0], norm_mix_gain[0:1], positions[:, :, None], invf, rwi)
    dec, qd, kd, chunk_decay = _retention_constants()
    y = _ret_call(q, k, v, sg, ret_norm_gain[0].reshape(1, RET_HEADS * RET_V), dec, qd, kd,
                  chunk_decay)
    x = _mlp_call(x, y, ada[0], norm_mlp_gain[0:1], rwo, w1, w2, 0)

    fb3 = jnp.pad(jnp.tile(forget_bias, 3), (0, LANES - BIAS_ONE0))[None, :]
    fb_col = forget_bias[:, None]
    kg = jnp.tile(k_norm_gain, FOX_HEADS)[None, :]
    ks, kb, vt, f2t = _kv_call(x, ada_kv, kv_norm_gain[None, :], kvw, wf3, wft, fb3, fb_col, kg)

    qg = jnp.tile(q_norm_gain[0], FOX_HEADS)[None, :]
    qf, sog = _fox_in_call(x, ada[1], norm_mix_gain[1:2], fwi, qg)
    y = _attn_call(qf, ks, kb, vt, f2t, sog)
    x = _mlp_call(x, y, ada[1], norm_mlp_gain[1:2], fwo, w1, w2, 1)
    return x
```

```python
import functools
import math

import jax
import jax.numpy as jnp
import numpy as np
from jax import lax
from jax.experimental import pallas as pl
from jax.experimental.pallas import tpu as pltpu

F32 = jnp.float32
BF = jnp.bfloat16

D_MODEL = 1024
RET_HEADS = 4
RET_QK = 256
RET_V = 512
RET_CHUNK = 256
FOX_HEADS = 16
FOX_DH = 64
D_FF = 4096
EPS = 1e-6
ROPE_BASE = 10000.0
LOG2E = math.log2(math.e)
NEG_BIG = -1e30
LANES = 128
MXU_DIM = 256
VMEM_LIMIT = 48 * 1024 * 1024

BIAS_ONE0 = 3 * FOX_HEADS


def _params(sem):
    return pltpu.CompilerParams(dimension_semantics=sem, vmem_limit_bytes=VMEM_LIMIT)


def _resident(shape, index_map):
    return pl.BlockSpec(shape, index_map, pipeline_mode=pl.Buffered(1))


def _ln_mod(x, gain, shift, scale):
    ms = jnp.mean(x * x, axis=-1, keepdims=True)
    y = x * lax.rsqrt(ms + EPS)
    return (y * gain) * (1.0 + scale) + shift


def _dot(a, b):
    return jnp.dot(a, b, preferred_element_type=F32)


def _dot_nt(a, b):
    return lax.dot_general(a, b, (((1,), (1,)), ((), ())), preferred_element_type=F32)


def _split3(a):
    hi = a.astype(BF).astype(F32)
    r1 = a - hi
    mid = r1.astype(BF).astype(F32)
    lo = (r1 - mid).astype(BF).astype(F32)
    return hi, mid, lo


def _ada_kernel(c_ref, w_ref, b_ref, o_ref):
    c = c_ref[...]
    ca = c * jax.nn.sigmoid(c)
    o_ref[...] = _dot(ca.astype(BF), w_ref[...].astype(BF)) + b_ref[...]


def _ada_call(c, w, b):
    L, Dm, N = w.shape
    B = c.shape[0]
    tn = 1024
    return pl.pallas_call(
        _ada_kernel,
        grid=(L, N // tn),
        in_specs=[
            pl.BlockSpec((B, Dm), lambda l, n: (0, 0)),
            pl.BlockSpec((None, Dm, tn), lambda l, n: (l, 0, n)),
            pl.BlockSpec((None, 1, tn), lambda l, n: (l, 0, n)),
        ],
        out_specs=pl.BlockSpec((None, B, tn), lambda l, n: (l, 0, n)),
        out_shape=jax.ShapeDtypeStruct((L, B, N), F32),
        compiler_params=_params(("arbitrary", "arbitrary")),
        name="ada",
    )(c, w, b)


def _ret_in_kernel(x_ref, ada_ref, gain_ref, pos_ref, invf_ref, w_ref,
                   q_ref, k_ref, v_ref, sg_ref):
    ada = ada_ref[0]
    mod = gain_ref[...] * (1.0 + ada[1:2])
    half = RET_QK // 2
    qk_w = RET_HEADS * RET_QK
    v_w = RET_HEADS * RET_V
    for sub in range(RET_IN_SUB):
        rows = slice(sub * RET_IN_TM, (sub + 1) * RET_IN_TM)
        x = x_ref[0, rows, :]
        xn = x * lax.rsqrt(jnp.mean(x * x, axis=-1, keepdims=True) + EPS)
        h = (xn * mod + ada[0:1]).astype(BF)
        v_ref[0, rows, :] = _dot(h, w_ref[:, 2 * qk_w: 2 * qk_w + v_w]).astype(BF)
        g = _dot(h, w_ref[:, 2 * qk_w + v_w:])
        sg_ref[0, rows, :] = (g * jax.nn.sigmoid(g)).astype(BF)
        ang = pos_ref[0, rows, :].astype(F32) * invf_ref[...]
        cos = jnp.cos(ang)
        sin = jnp.sin(ang)
        for dst, base, mul in ((q_ref, 0, 1.0), (k_ref, qk_w, RET_QK ** -0.5)):
            qk = _dot(h, w_ref[:, base: base + qk_w])
            for hh in range(RET_HEADS):
                lo = hh * RET_QK
                x1 = qk[:, lo: lo + half]
                x2 = qk[:, lo + half: lo + RET_QK]
                dst[0, rows, lo: lo + half] = ((x1 * cos - x2 * sin) * mul).astype(BF)
                dst[0, rows, lo + half: lo + RET_QK] = ((x1 * sin + x2 * cos) * mul).astype(BF)


RET_IN_TM = 256
RET_IN_SUB = 2


def _ret_in_call(x, ada, gain, pos, invf, w):
    B, S, Dm = x.shape
    tm = RET_IN_TM * RET_IN_SUB
    qk_w = RET_HEADS * RET_QK
    v_w = RET_HEADS * RET_V
    tok = lambda n: pl.BlockSpec((1, tm, n), lambda b, s: (b, s, 0))
    return pl.pallas_call(
        _ret_in_kernel,
        grid=(B, S // tm),
        in_specs=[
            tok(Dm),
            pl.BlockSpec((1, 6, Dm), lambda b, s: (b, 0, 0)),
            pl.BlockSpec((1, Dm), lambda b, s: (0, 0)),
            tok(1),
            pl.BlockSpec((1, RET_QK // 2), lambda b, s: (0, 0)),
            _resident((None, Dm, 2 * qk_w + 2 * v_w), lambda b, s: (0, 0, 0)),
        ],
        out_specs=[tok(qk_w), tok(qk_w), tok(v_w), tok(v_w)],
        out_shape=[
            jax.ShapeDtypeStruct((B, S, qk_w), BF),
            jax.ShapeDtypeStruct((B, S, qk_w), BF),
            jax.ShapeDtypeStruct((B, S, v_w), BF),
            jax.ShapeDtypeStruct((B, S, v_w), BF),
        ],
        compiler_params=_params(("arbitrary", "arbitrary")),
        name="ret_in",
    )(x, ada, gain, pos, invf, w)


RET_BATCH = 2


def _ret_kernel(chunk_decay, q_ref, k_ref, v_ref, sg_ref, gn_ref, dec_ref, qd_ref, kd_ref,
                y_ref, state_ref):
    @pl.when(pl.program_id(1) == 0)
    def _():
        state_ref[...] = jnp.zeros_like(state_ref)

    for bb in range(RET_BATCH):
        for hh in range(RET_HEADS):
            qs = slice(hh * RET_QK, (hh + 1) * RET_QK)
            vs = slice(hh * RET_V, (hh + 1) * RET_V)
            q = q_ref[bb, :, qs]
            k = k_ref[bb, :, qs]
            v = v_ref[bb, :, vs]
            s = _dot_nt(q, k) * dec_ref[hh]
            intra = _dot(s.astype(BF), v)
            st = state_ref[bb, hh]
            cross = _dot(q, st.astype(BF)) * qd_ref[hh]
            kdt = (k.astype(F32) * kd_ref[hh]).T.astype(BF)
            state_ref[bb, hh] = st * chunk_decay[hh] + _dot(kdt, v)
            y = intra + cross
            ms = jnp.mean(y * y, axis=-1, keepdims=True)
            yn = ((y * lax.rsqrt(ms + EPS)) * gn_ref[:, vs]).astype(BF)
            y_ref[bb, :, vs] = sg_ref[bb, :, vs] * yn


def _ret_call(q, k, v, sg, gn, dec, qd, kd, chunk_decay):
    B, S, qk_w = q.shape
    v_w = v.shape[-1]
    tok = lambda n: pl.BlockSpec((RET_BATCH, RET_CHUNK, n), lambda b, s: (b, s, 0))
    const = lambda shape: pl.BlockSpec(shape, lambda b, s: (0,) * len(shape))
    return pl.pallas_call(
        functools.partial(_ret_kernel, chunk_decay),
        grid=(B // RET_BATCH, S // RET_CHUNK),
        in_specs=[
            tok(qk_w), tok(qk_w), tok(v_w), tok(v_w),
            const((1, v_w)),
            const((RET_HEADS, RET_CHUNK, RET_CHUNK)),
            const((RET_HEADS, RET_CHUNK, 1)),
            const((RET_HEADS, RET_CHUNK, 1)),
        ],
        out_specs=tok(v_w),
        out_shape=jax.ShapeDtypeStruct((B, S, v_w), BF),
        scratch_shapes=[pltpu.VMEM((RET_BATCH, RET_HEADS, RET_QK, RET_V), F32)],
        compiler_params=_params(("arbitrary", "arbitrary")),
        name="retention",
    )(q, k, v, sg, gn, dec, qd, kd)


def _mlp_kernel(x_ref, y_ref, ada_ref, gain_ref, wo_ref, w1_ref, w2_ref, o_ref, hid_ref):
    ada = ada_ref[0]
    x1 = x_ref[0] + ada[2:3] * _dot(y_ref[0], wo_ref[...])
    h = _ln_mod(x1, gain_ref[...], ada[3:4], ada[4:5]).astype(BF)
    fc = 1024
    for c in range(D_FF // fc):
        a = jnp.maximum(_dot(h, w1_ref[:, c * fc:(c + 1) * fc]), 0.0)
        hid_ref[:, c * fc:(c + 1) * fc] = (a * a).astype(BF)
    o_ref[0] = x1 + ada[5:6] * _dot(hid_ref[...], w2_ref[...])


def _mlp_call(x, y, ada, gain, wo, w1, w2, layer):
    B, S, Dm = x.shape
    Ky = y.shape[-1]
    tm = 256
    tok = lambda n: pl.BlockSpec((1, tm, n), lambda b, s: (b, s, 0))
    return pl.pallas_call(
        _mlp_kernel,
        grid=(B, S // tm),
        in_specs=[
            tok(Dm), tok(Ky),
            pl.BlockSpec((1, 6, Dm), lambda b, s: (b, 0, 0)),
            pl.BlockSpec((1, Dm), lambda b, s: (0, 0)),
            _resident((None, Ky, Dm), lambda b, s: (0, 0, 0)),
            _resident((None, Dm, D_FF), lambda b, s: (layer, 0, 0)),
            _resident((None, D_FF, Dm), lambda b, s: (layer, 0, 0)),
        ],
        out_specs=tok(Dm),
        out_shape=jax.ShapeDtypeStruct((B, S, Dm), F32),
        scratch_shapes=[pltpu.VMEM((tm, D_FF), BF)],
        compiler_params=_params(("arbitrary", "arbitrary")),
        name=f"mlp{layer}",
    )(x, y, ada, gain, wo, w1, w2)


def _group_ones(n, group):
    r = lax.broadcasted_iota(jnp.int32, (n, n), 0) // group
    c = lax.broadcasted_iota(jnp.int32, (n, n), 1) // group
    return jnp.where(r == c, 1.0, 0.0).astype(BF)


def _head_rms(a, ones_blk):
    ss = _dot((a * a).astype(BF), ones_blk)
    return lax.rsqrt(ss * (1.0 / FOX_DH) + EPS)


def _log_sigmoid(z):
    return jnp.minimum(z, 0.0) - jnp.log(1.0 + jnp.exp(-jnp.abs(z)))


ATT_T = 256
VT_PAD = 16
VT_ROWS = FOX_DH + VT_PAD


KVQ_SUB = 2
KVQ_STEP = KVQ_SUB * ATT_T


def _kvq_kernel(x_ref, adakv_ref, ada_ref, gkv_ref, gq_ref, wkv_ref, wq_ref, wf3_ref, fb3_ref,
                kg_ref, qg_ref,
                k_ref, kb_ref, vt_ref, f2t_ref, q_ref, sog_ref, carry_row):
    @pl.when(pl.program_id(1) == 0)
    def _():
        carry_row[...] = jnp.zeros_like(carry_row)

    tm = ATT_T
    adakv = adakv_ref[0]
    ada = ada_ref[0]
    mod_kv = gkv_ref[...] * (1.0 + adakv[1:2])
    mod_q = gq_ref[...] * (1.0 + ada[1:2])
    ones_blk = _group_ones(MXU_DIM, FOX_DH)
    qscale = (FOX_DH ** -0.5) * LOG2E
    ri = lax.broadcasted_iota(jnp.int32, (tm, tm), 0)
    ci = lax.broadcasted_iota(jnp.int32, (tm, tm), 1)
    lower = jnp.where(ci <= ri, 1.0, 0.0).astype(BF)
    lane = lax.broadcasted_iota(jnp.int32, (tm, LANES), 1)
    ones_lane = jnp.where(lane < BIAS_ONE0 + 3, 1.0, 0.0)
    pad_rows = jnp.where(lax.broadcasted_iota(jnp.int32, (VT_PAD, tm), 0) == 0, 1.0, 0.0).astype(BF)

    for sub in range(KVQ_SUB):
        rows = slice(sub * tm, (sub + 1) * tm)
        x = x_ref[0, rows, :]
        xn = x * lax.rsqrt(jnp.mean(x * x, axis=-1, keepdims=True) + EPS)
        h = (xn * mod_kv + adakv[0:1]).astype(BF)
        hq = (xn * mod_q + ada[0:1]).astype(BF)

        kv = _dot(h, wkv_ref[...])
        for sl in range(D_MODEL // MXU_DIM):
            cs = slice(sl * MXU_DIM, (sl + 1) * MXU_DIM)
            a = kv[:, cs]
            k_ref[0, rows, cs] = (a * _head_rms(a, ones_blk) * kg_ref[:, cs]).astype(BF)
        for c in range(2):
            vt = kv[:, D_MODEL + c * 512: D_MODEL + (c + 1) * 512].T.astype(BF)
            for i in range(512 // FOX_DH):
                r0 = (c * (512 // FOX_DH) + i) * VT_ROWS
                vt_ref[0, sub, r0:r0 + FOX_DH, :] = vt[i * FOX_DH:(i + 1) * FOX_DH]
                vt_ref[0, sub, r0 + FOX_DH:r0 + VT_ROWS, :] = pad_rows

        hi, mid, lo = _split3(_log_sigmoid(_dot(h, wf3_ref[...]) + fb3_ref[...]))
        cum = ((_dot(lower, hi.astype(BF)) + _dot(lower, mid.astype(BF)))
               + _dot(lower, lo.astype(BF)) + carry_row[...])
        carry_row[...] = cum[tm - 1:tm, :]
        f2 = cum * LOG2E
        hi, mid, lo = _split3(f2)
        kb = jnp.where(lane < FOX_HEADS, -hi,
                       jnp.where(lane < 2 * FOX_HEADS, -mid,
                                 jnp.where(lane < BIAS_ONE0, -lo, ones_lane)))
        kb_ref[0, rows, :] = kb.astype(BF)
        f2t_ref[0, :, rows] = f2.T[:FOX_HEADS, :]

        qo = _dot(hq, wq_ref[...])
        for sl in range(D_MODEL // MXU_DIM):
            cs = slice(sl * MXU_DIM, (sl + 1) * MXU_DIM)
            a = qo[:, cs]
            q_ref[0, rows, cs] = (a * _head_rms(a, ones_blk) * (qg_ref[:, cs] * qscale)).astype(BF)
        sog_ref[0, rows, :] = jax.nn.sigmoid(qo[:, D_MODEL:]).astype(BF)


def _kvq_call(x, ada_kv, ada, gain_kv, gain_q, wkv, wq, wf3, fb3, kg, qg):
    B, S, Dm = x.shape
    ts = KVQ_STEP
    tok = lambda n: pl.BlockSpec((1, ts, n), lambda b, s: (b, s, 0))
    const = lambda shape: pl.BlockSpec(shape, lambda b, s: (0,) * len(shape))
    return pl.pallas_call(
        _kvq_kernel,
        grid=(B, S // ts),
        in_specs=[
            tok(Dm),
            pl.BlockSpec((1, 2, Dm), lambda b, s: (b, 0, 0)),
            pl.BlockSpec((1, 6, Dm), lambda b, s: (b, 0, 0)),
            const((1, Dm)), const((1, Dm)),
            _resident((Dm, 2 * Dm), lambda b, s: (0, 0)),
            _resident((None, Dm, 2 * Dm), lambda b, s: (0, 0, 0)),
            const((Dm, LANES)),
            const((1, LANES)),
            const((1, Dm)), const((1, Dm)),
        ],
        out_specs=[
            tok(Dm), tok(LANES),
            pl.BlockSpec((1, KVQ_SUB, FOX_HEADS * VT_ROWS, ATT_T), lambda b, s: (b, s, 0, 0)),
            pl.BlockSpec((1, FOX_HEADS, ts), lambda b, s: (b, 0, s)),
            tok(Dm), tok(Dm),
        ],
        out_shape=[
            jax.ShapeDtypeStruct((B, S, Dm), BF),
            jax.ShapeDtypeStruct((B, S, LANES), BF),
            jax.ShapeDtypeStruct((B, S // ATT_T, FOX_HEADS * VT_ROWS, ATT_T), BF),
            jax.ShapeDtypeStruct((B, FOX_HEADS, S), F32),
            jax.ShapeDtypeStruct((B, S, Dm), BF),
            jax.ShapeDtypeStruct((B, S, Dm), BF),
        ],
        scratch_shapes=[pltpu.VMEM((1, LANES), F32)],
        compiler_params=_params(("arbitrary", "arbitrary")),
        name="kvq",
    )(x, ada_kv, ada, gain_kv, gain_q, wkv, wq, wf3, fb3, kg, qg)


ATT_LOOKAHEAD = 8


def _attn_block(j, masked, k_ref, kb_ref, vt_ref, qa_ref, m_ref, acc_ref):
    t = ATT_T
    off = pl.multiple_of(j * t, t)
    kbias = kb_ref[0, pl.ds(off, t), :]
    if masked:
        causal = (lax.broadcasted_iota(jnp.int32, (t, t), 0)
                  <= lax.broadcasted_iota(jnp.int32, (t, t), 1))
    kcat = {}
    scores = {}
    for g in range(FOX_HEADS + ATT_LOOKAHEAD):
        if g < FOX_HEADS:
            p = g // 2
            if p not in kcat:
                kcat[p] = jnp.concatenate(
                    [k_ref[0, pl.ds(off, t), p * LANES:(p + 1) * LANES], kbias], axis=1)
            scores[g] = _dot(kcat[p], qa_ref[g])
        hd = g - ATT_LOOKAHEAD
        if hd >= 0:
            s = scores.pop(hd)
            if masked:
                s = jnp.where(causal, s, NEG_BIG)
            m = m_ref[hd:hd + 1, :]
            m_new = jnp.maximum(m, jnp.max(s, axis=0, keepdims=True))
            pr = jnp.exp2(s - m_new)
            alpha = jnp.exp2(m - m_new)
            m_ref[hd:hd + 1, :] = m_new
            vth = vt_ref[0, j, hd * VT_ROWS:(hd + 1) * VT_ROWS, :]
            acc_ref[hd] = alpha * acc_ref[hd] + _dot(vth, pr.astype(BF))


def _attn_kernel(q_ref, k_ref, kb_ref, vt_ref, f2t_ref, sog_ref, y_ref,
                 qa_ref, m_ref, acc_ref):
    t = ATT_T
    qi = pl.program_id(1)
    row = lax.broadcasted_iota(jnp.int32, (LANES, t), 0)
    zeros_half = jnp.zeros((FOX_DH, t), F32)

    for p in range(FOX_HEADS // 2):
        cs = slice(p * LANES, (p + 1) * LANES)
        qt = q_ref[0, :, cs].astype(F32).T
        for hh in range(2):
            hd = 2 * p + hh
            fhi, fmid, flo = _split3(f2t_ref[0, hd:hd + 1, :])
            bias = (jnp.where(row == hd, 1.0, 0.0)
                    + jnp.where(row == FOX_HEADS + hd, 1.0, 0.0)
                    + jnp.where(row == 2 * FOX_HEADS + hd, 1.0, 0.0)
                    + jnp.where(row == BIAS_ONE0, fhi, 0.0)
                    + jnp.where(row == BIAS_ONE0 + 1, fmid, 0.0)
                    + jnp.where(row == BIAS_ONE0 + 2, flo, 0.0))
            if hh == 0:
                top = jnp.concatenate([qt[:FOX_DH], zeros_half], axis=0)
            else:
                top = jnp.concatenate([zeros_half, qt[FOX_DH:]], axis=0)
            qa_ref[hd] = jnp.concatenate([top, bias], axis=0).astype(BF)

    m_ref[...] = jnp.full(m_ref.shape, NEG_BIG, F32)
    acc_ref[...] = jnp.zeros_like(acc_ref)

    refs = (k_ref, kb_ref, vt_ref, qa_ref, m_ref, acc_ref)

    def body(j, carry):
        _attn_block(j, False, *refs)
        return carry

    lax.fori_loop(0, qi, body, 0)
    _attn_block(qi, True, *refs)

    for p in range(FOX_HEADS // 2):
        cs = slice(p * LANES, (p + 1) * LANES)
        halves = []
        for hd in (2 * p, 2 * p + 1):
            num = acc_ref[hd, :FOX_DH, :]
            den = acc_ref[hd, FOX_DH:FOX_DH + 1, :]
            halves.append(num * (1.0 / den))
        ot = jnp.concatenate(halves, axis=0)
        y_ref[0, :, cs] = (sog_ref[0, :, cs].astype(F32) * ot.T).astype(BF)


def _attn_call(q, k, kb, vt, f2t, sog):
    B, S, Dm = q.shape
    t = ATT_T
    tok = lambda n: pl.BlockSpec((1, t, n), lambda b, s: (b, s, 0))
    return pl.pallas_call(
        _attn_kernel,
        grid=(B, S // t),
        in_specs=[
            tok(Dm),
            pl.BlockSpec((1, S, Dm), lambda b, s: (b, 0, 0)),
            pl.BlockSpec((1, S, LANES), lambda b, s: (b, 0, 0)),
            pl.BlockSpec((1, S // t, FOX_HEADS * VT_ROWS, t), lambda b, s: (b, 0, 0, 0)),
            pl.BlockSpec((1, FOX_HEADS, t), lambda b, s: (b, 0, s)),
            tok(Dm),
        ],
        out_specs=tok(Dm),
        out_shape=jax.ShapeDtypeStruct((B, S, Dm), BF),
        scratch_shapes=[
            pltpu.VMEM((FOX_HEADS, 2 * LANES, t), BF),
            pltpu.VMEM((FOX_HEADS, t), F32),
            pltpu.VMEM((FOX_HEADS, VT_ROWS, t), F32),
        ],
        compiler_params=_params(("arbitrary", "arbitrary")),
        name="fox_attn",
    )(q, k, kb, vt, f2t, sog)


def _retention_constants():
    gam = 1.0 - np.power(2.0, -5.0 - np.arange(RET_HEADS, dtype=np.float64))
    log_g = np.log(gam)
    idx = np.arange(RET_CHUNK, dtype=np.float64)
    rel = idx[:, None] - idx[None, :]
    dec = np.where(rel >= 0, np.exp(log_g[:, None, None] * np.maximum(rel, 0.0)), 0.0)
    qd = np.exp(log_g[:, None] * (idx + 1.0))[:, :, None]
    kd = np.exp(log_g[:, None] * (RET_CHUNK - 1.0 - idx))[:, :, None]
    chunk_decay = tuple(float(v) for v in np.exp(log_g * RET_CHUNK))
    return (jnp.asarray(dec, F32), jnp.asarray(qd, F32), jnp.asarray(kd, F32), chunk_decay)


def kernel(x, c, positions, norm_mix_gain, norm_mlp_gain, w_ada, b_ada, w_mlp_in, w_mlp_out,
           ret_w_in, ret_norm_gain, ret_w_out, kv_norm_gain, kv_w_ada, kv_b_ada, kv_w,
           forget_bias, k_norm_gain, fox_w_in, q_norm_gain, fox_w_out):
    B, S, Dm = x.shape

    ada = _ada_call(c, w_ada, b_ada[:, None, :])
    ada = ada.reshape(2, B, 6, Dm)
    ada_kv = _ada_call(c, kv_w_ada[None], kv_b_ada[None, None, :])
    ada_kv = ada_kv.reshape(B, 2, Dm)

    w1 = w_mlp_in.astype(BF)
    w2 = w_mlp_out.astype(BF)
    rwi = ret_w_in.astype(BF)
    rwo = ret_w_out.astype(BF)
    kvw = kv_w[:, :2 * Dm].astype(BF)
    wf = kv_w[:, 2 * Dm:]
    wf3 = jnp.pad(jnp.tile(wf, (1, 3)), ((0, 0), (0, LANES - BIAS_ONE0))).astype(BF)
    fwi = fox_w_in.astype(BF)
    fwo = fox_w_out.astype(BF)

    half = RET_QK // 2
    invf = (ROPE_BASE ** (-jnp.arange(half, dtype=F32) / half))[None, :]
    q, k, v, sg = _ret_in_call(x, ada[0], norm_mix_gain[0:1], positions[:, :, None], invf, rwi)
    dec, qd, kd, chunk_decay = _retention_constants()
    y = _ret_call(q, k, v, sg, ret_norm_gain[0].reshape(1, RET_HEADS * RET_V), dec, qd, kd,
                  chunk_decay)
    x = _mlp_call(x, y, ada[0], norm_mlp_gain[0:1], rwo, w1, w2, 0)

    fb3 = jnp.pad(jnp.tile(forget_bias, 3), (0, LANES - BIAS_ONE0))[None, :]
    kg = jnp.tile(k_norm_gain, FOX_HEADS)[None, :]
    qg = jnp.tile(q_norm_gain[0], FOX_HEADS)[None, :]
    ks, kb, vt, f2t, qf, sog = _kvq_call(x, ada_kv, ada[1], kv_norm_gain[None, :],
                                         norm_mix_gain[1:2], kvw, fwi, wf3, fb3, kg, qg)
    y = _attn_call(qf, ks, kb, vt, f2t, sog)
    x = _mlp_call(x, y, ada[1], norm_mlp_gain[1:2], fwo, w1, w2, 1)
    return x
```

```python
import functools
import math

import jax
import jax.numpy as jnp
import numpy as np
from jax import lax
from jax.experimental import pallas as pl
from jax.experimental.pallas import tpu as pltpu

F32 = jnp.float32
BF = jnp.bfloat16

D_MODEL = 1024
RET_HEADS = 4
RET_QK = 256
RET_V = 512
RET_CHUNK = 256
FOX_HEADS = 16
FOX_DH = 64
D_FF = 4096
EPS = 1e-6
ROPE_BASE = 10000.0
LOG2E = math.log2(math.e)
NEG_BIG = -1e30
LANES = 128
MXU_DIM = 256
VMEM_LIMIT = 48 * 1024 * 1024

BIAS_ONE0 = 3 * FOX_HEADS


def _params(sem):
    return pltpu.CompilerParams(dimension_semantics=sem, vmem_limit_bytes=VMEM_LIMIT)


def _resident(shape, index_map):
    return pl.BlockSpec(shape, index_map, pipeline_mode=pl.Buffered(1))


def _ln_mod(x, gain, shift, scale):
    ms = jnp.mean(x * x, axis=-1, keepdims=True)
    y = x * lax.rsqrt(ms + EPS)
    return (y * gain) * (1.0 + scale) + shift


def _dot(a, b):
    return jnp.dot(a, b, preferred_element_type=F32)


def _dot_nt(a, b):
    return lax.dot_general(a, b, (((1,), (1,)), ((), ())), preferred_element_type=F32)


def _split3(a):
    hi = a.astype(BF).astype(F32)
    r1 = a - hi
    mid = r1.astype(BF).astype(F32)
    lo = (r1 - mid).astype(BF).astype(F32)
    return hi, mid, lo


def _ada_kernel(c_ref, w_ref, b_ref, o_ref):
    c = c_ref[...]
    ca = c * jax.nn.sigmoid(c)
    o_ref[...] = _dot(ca.astype(BF), w_ref[...].astype(BF)) + b_ref[...]


def _ada_call(c, w, b):
    L, Dm, N = w.shape
    B = c.shape[0]
    tn = 1024
    return pl.pallas_call(
        _ada_kernel,
        grid=(L, N // tn),
        in_specs=[
            pl.BlockSpec((B, Dm), lambda l, n: (0, 0)),
            pl.BlockSpec((None, Dm, tn), lambda l, n: (l, 0, n)),
            pl.BlockSpec((None, 1, tn), lambda l, n: (l, 0, n)),
        ],
        out_specs=pl.BlockSpec((None, B, tn), lambda l, n: (l, 0, n)),
        out_shape=jax.ShapeDtypeStruct((L, B, N), F32),
        compiler_params=_params(("arbitrary", "arbitrary")),
        name="ada",
    )(c, w, b)


def _ret_in_kernel(x_ref, ada_ref, gain_ref, pos_ref, invf_ref, w_ref,
                   q_ref, k_ref, v_ref, sg_ref):
    ada = ada_ref[0]
    mod = gain_ref[...] * (1.0 + ada[1:2])
    half = RET_QK // 2
    qk_w = RET_HEADS * RET_QK
    v_w = RET_HEADS * RET_V
    for sub in range(RET_IN_SUB):
        rows = slice(sub * RET_IN_TM, (sub + 1) * RET_IN_TM)
        x = x_ref[0, rows, :]
        xn = x * lax.rsqrt(jnp.mean(x * x, axis=-1, keepdims=True) + EPS)
        h = (xn * mod + ada[0:1]).astype(BF)
        v_ref[0, rows, :] = _dot(h, w_ref[:, 2 * qk_w: 2 * qk_w + v_w]).astype(BF)
        g = _dot(h, w_ref[:, 2 * qk_w + v_w:])
        sg_ref[0, rows, :] = (g * jax.nn.sigmoid(g)).astype(BF)
        ang = pos_ref[0, rows, :].astype(F32) * invf_ref[...]
        cos = jnp.cos(ang)
        sin = jnp.sin(ang)
        for dst, base, mul in ((q_ref, 0, 1.0), (k_ref, qk_w, RET_QK ** -0.5)):
            qk = _dot(h, w_ref[:, base: base + qk_w])
            for hh in range(RET_HEADS):
                lo = hh * RET_QK
                x1 = qk[:, lo: lo + half]
                x2 = qk[:, lo + half: lo + RET_QK]
                dst[0, rows, lo: lo + half] = ((x1 * cos - x2 * sin) * mul).astype(BF)
                dst[0, rows, lo + half: lo + RET_QK] = ((x1 * sin + x2 * cos) * mul).astype(BF)


RET_IN_TM = 256
RET_IN_SUB = 2


def _ret_in_call(x, ada, gain, pos, invf, w):
    B, S, Dm = x.shape
    tm = RET_IN_TM * RET_IN_SUB
    qk_w = RET_HEADS * RET_QK
    v_w = RET_HEADS * RET_V
    tok = lambda n: pl.BlockSpec((1, tm, n), lambda b, s: (b, s, 0))
    return pl.pallas_call(
        _ret_in_kernel,
        grid=(B, S // tm),
        in_specs=[
            tok(Dm),
            pl.BlockSpec((1, 6, Dm), lambda b, s: (b, 0, 0)),
            pl.BlockSpec((1, Dm), lambda b, s: (0, 0)),
            tok(1),
            pl.BlockSpec((1, RET_QK // 2), lambda b, s: (0, 0)),
            _resident((None, Dm, 2 * qk_w + 2 * v_w), lambda b, s: (0, 0, 0)),
        ],
        out_specs=[tok(qk_w), tok(qk_w), tok(v_w), tok(v_w)],
        out_shape=[
            jax.ShapeDtypeStruct((B, S, qk_w), BF),
            jax.ShapeDtypeStruct((B, S, qk_w), BF),
            jax.ShapeDtypeStruct((B, S, v_w), BF),
            jax.ShapeDtypeStruct((B, S, v_w), BF),
        ],
        compiler_params=_params(("arbitrary", "arbitrary")),
        name="ret_in",
    )(x, ada, gain, pos, invf, w)


RET_BATCH = 2


def _ret_kernel(chunk_decay, q_ref, k_ref, v_ref, sg_ref, gn_ref, dec_ref, qd_ref, kd_ref,
                y_ref, state_ref):
    @pl.when(pl.program_id(1) == 0)
    def _():
        state_ref[...] = jnp.zeros_like(state_ref)

    for bb in range(RET_BATCH):
        for hh in range(RET_HEADS):
            qs = slice(hh * RET_QK, (hh + 1) * RET_QK)
            vs = slice(hh * RET_V, (hh + 1) * RET_V)
            q = q_ref[bb, :, qs]
            k = k_ref[bb, :, qs]
            v = v_ref[bb, :, vs]
            s = _dot_nt(q, k) * dec_ref[hh]
            intra = _dot(s.astype(BF), v)
            st = state_ref[bb, hh]
            cross = _dot(q, st.astype(BF)) * qd_ref[hh]
            kdt = (k.astype(F32) * kd_ref[hh]).T.astype(BF)
            state_ref[bb, hh] = st * chunk_decay[hh] + _dot(kdt, v)
            y = intra + cross
            ms = jnp.mean(y * y, axis=-1, keepdims=True)
            yn = ((y * lax.rsqrt(ms + EPS)) * gn_ref[:, vs]).astype(BF)
            y_ref[bb, :, vs] = sg_ref[bb, :, vs] * yn


def _ret_call(q, k, v, sg, gn, dec, qd, kd, chunk_decay):
    B, S, qk_w = q.shape
    v_w = v.shape[-1]
    tok = lambda n: pl.BlockSpec((RET_BATCH, RET_CHUNK, n), lambda b, s: (b, s, 0))
    const = lambda shape: pl.BlockSpec(shape, lambda b, s: (0,) * len(shape))
    return pl.pallas_call(
        functools.partial(_ret_kernel, chunk_decay),
        grid=(B // RET_BATCH, S // RET_CHUNK),
        in_specs=[
            tok(qk_w), tok(qk_w), tok(v_w), tok(v_w),
            const((1, v_w)),
            const((RET_HEADS, RET_CHUNK, RET_CHUNK)),
            const((RET_HEADS, RET_CHUNK, 1)),
            const((RET_HEADS, RET_CHUNK, 1)),
        ],
        out_specs=tok(v_w),
        out_shape=jax.ShapeDtypeStruct((B, S, v_w), BF),
        scratch_shapes=[pltpu.VMEM((RET_BATCH, RET_HEADS, RET_QK, RET_V), F32)],
        compiler_params=_params(("arbitrary", "arbitrary")),
        name="retention",
    )(q, k, v, sg, gn, dec, qd, kd)


def _mlp_kernel(x_ref, y_ref, ada_ref, gain_ref, wo_ref, w1_ref, w2_ref, o_ref, hid_ref):
    ada = ada_ref[0]
    x1 = x_ref[0] + ada[2:3] * _dot(y_ref[0], wo_ref[...])
    h = _ln_mod(x1, gain_ref[...], ada[3:4], ada[4:5]).astype(BF)
    fc = 1024
    for c in range(D_FF // fc):
        a = jnp.maximum(_dot(h, w1_ref[:, c * fc:(c + 1) * fc]), 0.0)
        hid_ref[:, c * fc:(c + 1) * fc] = (a * a).astype(BF)
    o_ref[0] = x1 + ada[5:6] * _dot(hid_ref[...], w2_ref[...])


def _mlp_call(x, y, ada, gain, wo, w1, w2, layer):
    B, S, Dm = x.shape
    Ky = y.shape[-1]
    tm = 256
    tok = lambda n: pl.BlockSpec((1, tm, n), lambda b, s: (b, s, 0))
    return pl.pallas_call(
        _mlp_kernel,
        grid=(B, S // tm),
        in_specs=[
            tok(Dm), tok(Ky),
            pl.BlockSpec((1, 6, Dm), lambda b, s: (b, 0, 0)),
            pl.BlockSpec((1, Dm), lambda b, s: (0, 0)),
            _resident((None, Ky, Dm), lambda b, s: (0, 0, 0)),
            _resident((None, Dm, D_FF), lambda b, s: (layer, 0, 0)),
            _resident((None, D_FF, Dm), lambda b, s: (layer, 0, 0)),
        ],
        out_specs=tok(Dm),
        out_shape=jax.ShapeDtypeStruct((B, S, Dm), F32),
        scratch_shapes=[pltpu.VMEM((tm, D_FF), BF)],
        compiler_params=_params(("arbitrary", "arbitrary")),
        name=f"mlp{layer}",
    )(x, y, ada, gain, wo, w1, w2)


def _group_ones(n, group):
    r = lax.broadcasted_iota(jnp.int32, (n, n), 0) // group
    c = lax.broadcasted_iota(jnp.int32, (n, n), 1) // group
    return jnp.where(r == c, 1.0, 0.0).astype(BF)


def _head_rms(a, ones_blk):
    ss = _dot((a * a).astype(BF), ones_blk)
    return lax.rsqrt(ss * (1.0 / FOX_DH) + EPS)


def _log_sigmoid(z):
    return jnp.minimum(z, 0.0) - jnp.log(1.0 + jnp.exp(-jnp.abs(z)))


ATT_T = 256
VT_PAD = 16
VT_ROWS = FOX_DH + VT_PAD


KVQ_SUB = 2
KVQ_STEP = KVQ_SUB * ATT_T


def _kvq_kernel(x_ref, adakv_ref, ada_ref, gkv_ref, gq_ref, wkv_ref, wq_ref, wf3_ref, fb3_ref,
                kg_ref, qg_ref,
                k_ref, kb_ref, vt_ref, f2t_ref, q_ref, sog_ref, carry_row):
    @pl.when(pl.program_id(1) == 0)
    def _():
        carry_row[...] = jnp.zeros_like(carry_row)

    tm = ATT_T
    adakv = adakv_ref[0]
    ada = ada_ref[0]
    mod_kv = gkv_ref[...] * (1.0 + adakv[1:2])
    mod_q = gq_ref[...] * (1.0 + ada[1:2])
    ones_blk = _group_ones(MXU_DIM, FOX_DH)
    qscale = (FOX_DH ** -0.5) * LOG2E
    ri = lax.broadcasted_iota(jnp.int32, (tm, tm), 0)
    ci = lax.broadcasted_iota(jnp.int32, (tm, tm), 1)
    lower = jnp.where(ci <= ri, 1.0, 0.0).astype(BF)
    lane = lax.broadcasted_iota(jnp.int32, (tm, LANES), 1)
    ones_lane = jnp.where(lane < BIAS_ONE0 + 3, 1.0, 0.0)
    pad_rows = jnp.where(lax.broadcasted_iota(jnp.int32, (VT_PAD, tm), 0) == 0, 1.0, 0.0).astype(BF)

    for sub in range(KVQ_SUB):
        rows = slice(sub * tm, (sub + 1) * tm)
        x = x_ref[0, rows, :]
        xn = x * lax.rsqrt(jnp.mean(x * x, axis=-1, keepdims=True) + EPS)
        h = (xn * mod_kv + adakv[0:1]).astype(BF)
        hq = (xn * mod_q + ada[0:1]).astype(BF)

        kv = _dot(h, wkv_ref[...])
        for sl in range(D_MODEL // MXU_DIM):
            cs = slice(sl * MXU_DIM, (sl + 1) * MXU_DIM)
            a = kv[:, cs]
            k_ref[0, rows, cs] = (a * _head_rms(a, ones_blk) * kg_ref[:, cs]).astype(BF)
        for c in range(2):
            vt = kv[:, D_MODEL + c * 512: D_MODEL + (c + 1) * 512].T.astype(BF)
            for i in range(512 // FOX_DH):
                r0 = (c * (512 // FOX_DH) + i) * VT_ROWS
                vt_ref[0, sub, r0:r0 + FOX_DH, :] = vt[i * FOX_DH:(i + 1) * FOX_DH]
                vt_ref[0, sub, r0 + FOX_DH:r0 + VT_ROWS, :] = pad_rows

        hi, mid, lo = _split3(_log_sigmoid(_dot(h, wf3_ref[...]) + fb3_ref[...]))
        cum = ((_dot(lower, hi.astype(BF)) + _dot(lower, mid.astype(BF)))
               + _dot(lower, lo.astype(BF)) + carry_row[...])
        carry_row[...] = cum[tm - 1:tm, :]
        f2 = cum * LOG2E
        hi, mid, lo = _split3(f2)
        kb = jnp.where(lane < FOX_HEADS, -hi,
                       jnp.where(lane < 2 * FOX_HEADS, -mid,
                                 jnp.where(lane < BIAS_ONE0, -lo, ones_lane)))
        kb_ref[0, rows, :] = kb.astype(BF)
        f2t_ref[0, :, rows] = f2.T[:FOX_HEADS, :]

        qo = _dot(hq, wq_ref[...])
        for sl in range(D_MODEL // MXU_DIM):
            cs = slice(sl * MXU_DIM, (sl + 1) * MXU_DIM)
            a = qo[:, cs]
            q_ref[0, rows, cs] = (a * _head_rms(a, ones_blk) * (qg_ref[:, cs] * qscale)).astype(BF)
        sog_ref[0, rows, :] = jax.nn.sigmoid(qo[:, D_MODEL:]).astype(BF)


def _kvq_call(x, ada_kv, ada, gain_kv, gain_q, wkv, wq, wf3, fb3, kg, qg):
    B, S, Dm = x.shape
    ts = KVQ_STEP
    tok = lambda n: pl.BlockSpec((1, ts, n), lambda b, s: (b, s, 0))
    const = lambda shape: pl.BlockSpec(shape, lambda b, s: (0,) * len(shape))
    return pl.pallas_call(
        _kvq_kernel,
        grid=(B, S // ts),
        in_specs=[
            tok(Dm),
            pl.BlockSpec((1, 2, Dm), lambda b, s: (b, 0, 0)),
            pl.BlockSpec((1, 6, Dm), lambda b, s: (b, 0, 0)),
            const((1, Dm)), const((1, Dm)),
            _resident((Dm, 2 * Dm), lambda b, s: (0, 0)),
            _resident((None, Dm, 2 * Dm), lambda b, s: (0, 0, 0)),
            const((Dm, LANES)),
            const((1, LANES)),
            const((1, Dm)), const((1, Dm)),
        ],
        out_specs=[
            tok(Dm), tok(LANES),
            pl.BlockSpec((1, KVQ_SUB, FOX_HEADS * VT_ROWS, ATT_T), lambda b, s: (b, s, 0, 0)),
            pl.BlockSpec((1, FOX_HEADS, ts), lambda b, s: (b, 0, s)),
            tok(Dm), tok(Dm),
        ],
        out_shape=[
            jax.ShapeDtypeStruct((B, S, Dm), BF),
            jax.ShapeDtypeStruct((B, S, LANES), BF),
            jax.ShapeDtypeStruct((B, S // ATT_T, FOX_HEADS * VT_ROWS, ATT_T), BF),
            jax.ShapeDtypeStruct((B, FOX_HEADS, S), F32),
            jax.ShapeDtypeStruct((B, S, Dm), BF),
            jax.ShapeDtypeStruct((B, S, Dm), BF),
        ],
        scratch_shapes=[pltpu.VMEM((1, LANES), F32)],
        compiler_params=_params(("arbitrary", "arbitrary")),
        name="kvq",
    )(x, ada_kv, ada, gain_kv, gain_q, wkv, wq, wf3, fb3, kg, qg)


ATT_SCORE_LEAD = 1


def _attn_scores(j, hd, slot, k_ref, kb_ref, qa_ref, s_ref):
    t = ATT_T
    off = pl.multiple_of(j * t, t)
    p = hd // 2
    kcat = jnp.concatenate([k_ref[0, pl.ds(off, t), p * LANES:(p + 1) * LANES],
                            kb_ref[0, pl.ds(off, t), :]], axis=1)
    s_ref[slot, hd] = _dot(kcat, qa_ref[hd])


def _attn_update(j, hd, slot, masked, vt_ref, s_ref, m_ref, acc_ref):
    t = ATT_T
    s = s_ref[slot, hd]
    if masked:
        causal = (lax.broadcasted_iota(jnp.int32, (t, t), 0)
                  <= lax.broadcasted_iota(jnp.int32, (t, t), 1))
        s = jnp.where(causal, s, NEG_BIG)
    m = m_ref[hd:hd + 1, :]
    m_new = jnp.maximum(m, jnp.max(s, axis=0, keepdims=True))
    pr = jnp.exp2(s - m_new)
    alpha = jnp.exp2(m - m_new)
    m_ref[hd:hd + 1, :] = m_new
    vth = vt_ref[0, j, hd * VT_ROWS:(hd + 1) * VT_ROWS, :]
    acc_ref[hd] = alpha * acc_ref[hd] + _dot(vth, pr.astype(BF))


def _attn_kernel(q_ref, k_ref, kb_ref, vt_ref, f2t_ref, sog_ref, y_ref,
                 qa_ref, s_ref, m_ref, acc_ref):
    t = ATT_T
    qi = pl.program_id(1)

    @pl.when(jnp.logical_and(pl.program_id(0) == 0, qi == 0))
    def _():
        row = lax.broadcasted_iota(jnp.int32, (2 * LANES, t), 0) - LANES
        for hd in range(FOX_HEADS):
            sel = (jnp.where(row == hd, 1.0, 0.0)
                   + jnp.where(row == FOX_HEADS + hd, 1.0, 0.0)
                   + jnp.where(row == 2 * FOX_HEADS + hd, 1.0, 0.0))
            qa_ref[hd] = sel.astype(BF)

    f_row0 = LANES + BIAS_ONE0
    f_pad = jnp.zeros((VT_PAD - 3, t), F32)
    for p in range(FOX_HEADS // 2):
        cs = slice(p * LANES, (p + 1) * LANES)
        qt = q_ref[0, :, cs].T
        for hh in range(2):
            hd = 2 * p + hh
            qa_ref[hd, hh * FOX_DH:(hh + 1) * FOX_DH, :] = qt[hh * FOX_DH:(hh + 1) * FOX_DH]
            fhi, fmid, flo = _split3(f2t_ref[0, hd:hd + 1, :])
            qa_ref[hd, f_row0:f_row0 + VT_PAD, :] = jnp.concatenate(
                [fhi, fmid, flo, f_pad], axis=0).astype(BF)

    m_ref[...] = jnp.full(m_ref.shape, NEG_BIG, F32)
    acc_ref[...] = jnp.zeros_like(acc_ref)

    def stage(j, cur):
        for g in range(FOX_HEADS + ATT_SCORE_LEAD):
            if g < FOX_HEADS:
                _attn_scores(j + 1, g, 1 - cur, k_ref, kb_ref, qa_ref, s_ref)
            if g >= ATT_SCORE_LEAD:
                _attn_update(j, g - ATT_SCORE_LEAD, cur, False, vt_ref, s_ref, m_ref, acc_ref)

    def last(cur):
        for hd in range(FOX_HEADS):
            _attn_update(qi, hd, cur, True, vt_ref, s_ref, m_ref, acc_ref)

    for hd in range(FOX_HEADS):
        _attn_scores(0, hd, 0, k_ref, kb_ref, qa_ref, s_ref)

    def body(i, carry):
        stage(2 * i, 0)
        stage(2 * i + 1, 1)
        return carry

    lax.fori_loop(0, lax.shift_right_logical(qi, 1), body, 0)
    odd = lax.bitwise_and(qi, 1) == 1

    @pl.when(odd)
    def _():
        stage(qi - 1, 0)
        last(1)

    @pl.when(jnp.logical_not(odd))
    def _():
        last(0)

    for p in range(FOX_HEADS // 2):
        cs = slice(p * LANES, (p + 1) * LANES)
        halves = []
        for hd in (2 * p, 2 * p + 1):
            num = acc_ref[hd, :FOX_DH, :]
            den = acc_ref[hd, FOX_DH:FOX_DH + 1, :]
            halves.append(num * (1.0 / den))
        ot = jnp.concatenate(halves, axis=0)
        y_ref[0, :, cs] = (sog_ref[0, :, cs].astype(F32) * ot.T).astype(BF)


def _attn_call(q, k, kb, vt, f2t, sog):
    B, S, Dm = q.shape
    t = ATT_T
    tok = lambda n: pl.BlockSpec((1, t, n), lambda b, s: (b, s, 0))
    return pl.pallas_call(
        _attn_kernel,
        grid=(B, S // t),
        in_specs=[
            tok(Dm),
            pl.BlockSpec((1, S, Dm), lambda b, s: (b, 0, 0)),
            pl.BlockSpec((1, S, LANES), lambda b, s: (b, 0, 0)),
            pl.BlockSpec((1, S // t, FOX_HEADS * VT_ROWS, t), lambda b, s: (b, 0, 0, 0)),
            pl.BlockSpec((1, FOX_HEADS, t), lambda b, s: (b, 0, s)),
            tok(Dm),
        ],
        out_specs=tok(Dm),
        out_shape=jax.ShapeDtypeStruct((B, S, Dm), BF),
        scratch_shapes=[
            pltpu.VMEM((FOX_HEADS, 2 * LANES, t), BF),
            pltpu.VMEM((2, FOX_HEADS, t, t), F32),
            pltpu.VMEM((FOX_HEADS, t), F32),
            pltpu.VMEM((FOX_HEADS, VT_ROWS, t), F32),
        ],
        compiler_params=_params(("arbitrary", "arbitrary")),
        name="fox_attn",
    )(q, k, kb, vt, f2t, sog)


def _retention_constants():
    gam = 1.0 - np.power(2.0, -5.0 - np.arange(RET_HEADS, dtype=np.float64))
    log_g = np.log(gam)
    idx = np.arange(RET_CHUNK, dtype=np.float64)
    rel = idx[:, None] - idx[None, :]
    dec = np.where(rel >= 0, np.exp(log_g[:, None, None] * np.maximum(rel, 0.0)), 0.0)
    qd = np.exp(log_g[:, None] * (idx + 1.0))[:, :, None]
    kd = np.exp(log_g[:, None] * (RET_CHUNK - 1.0 - idx))[:, :, None]
    chunk_decay = tuple(float(v) for v in np.exp(log_g * RET_CHUNK))
    return (jnp.asarray(dec, F32), jnp.asarray(qd, F32), jnp.asarray(kd, F32), chunk_decay)


def kernel(x, c, positions, norm_mix_gain, norm_mlp_gain, w_ada, b_ada, w_mlp_in, w_mlp_out,
           ret_w_in, ret_norm_gain, ret_w_out, kv_norm_gain, kv_w_ada, kv_b_ada, kv_w,
           forget_bias, k_norm_gain, fox_w_in, q_norm_gain, fox_w_out):
    B, S, Dm = x.shape

    ada = _ada_call(c, w_ada, b_ada[:, None, :])
    ada = ada.reshape(2, B, 6, Dm)
    ada_kv = _ada_call(c, kv_w_ada[None], kv_b_ada[None, None, :])
    ada_kv = ada_kv.reshape(B, 2, Dm)

    w1 = w_mlp_in.astype(BF)
    w2 = w_mlp_out.astype(BF)
    rwi = ret_w_in.astype(BF)
    rwo = ret_w_out.astype(BF)
    kvw = kv_w[:, :2 * Dm].astype(BF)
    wf = kv_w[:, 2 * Dm:]
    wf3 = jnp.pad(jnp.tile(wf, (1, 3)), ((0, 0), (0, LANES - BIAS_ONE0))).astype(BF)
    fwi = fox_w_in.astype(BF)
    fwo = fox_w_out.astype(BF)

    half = RET_QK // 2
    invf = (ROPE_BASE ** (-jnp.arange(half, dtype=F32) / half))[None, :]
    q, k, v, sg = _ret_in_call(x, ada[0], norm_mix_gain[0:1], positions[:, :, None], invf, rwi)
    dec, qd, kd, chunk_decay = _retention_constants()
    y = _ret_call(q, k, v, sg, ret_norm_gain[0].reshape(1, RET_HEADS * RET_V), dec, qd, kd,
                  chunk_decay)
    x = _mlp_call(x, y, ada[0], norm_mlp_gain[0:1], rwo, w1, w2, 0)

    fb3 = jnp.pad(jnp.tile(forget_bias, 3), (0, LANES - BIAS_ONE0))[None, :]
    kg = jnp.tile(k_norm_gain, FOX_HEADS)[None, :]
    qg = jnp.tile(q_norm_gain[0], FOX_HEADS)[None, :]
    ks, kb, vt, f2t, qf, sog = _kvq_call(x, ada_kv, ada[1], kv_norm_gain[None, :],
                                         norm_mix_gain[1:2], kvw, fwi, wf3, fb3, kg, qg)
    y = _attn_call(qf, ks, kb, vt, f2t, sog)
    x = _mlp_call(x, y, ada[1], norm_mlp_gain[1:2], fwo, w1, w2, 1)
    return x
```

```python
import functools
import math

import jax
import jax.numpy as jnp
import numpy as np
from jax import lax
from jax.experimental import pallas as pl
from jax.experimental.pallas import tpu as pltpu

F32 = jnp.float32
BF = jnp.bfloat16

D_MODEL = 1024
RET_HEADS = 4
RET_QK = 256
RET_V = 512
RET_CHUNK = 256
FOX_HEADS = 16
FOX_DH = 64
D_FF = 4096
EPS = 1e-6
ROPE_BASE = 10000.0
LOG2E = math.log2(math.e)
NEG_BIG = -1e30
LANES = 128
MXU_DIM = 256
VMEM_LIMIT = 48 * 1024 * 1024

BIAS_ONE0 = 3 * FOX_HEADS


def _params(sem):
    return pltpu.CompilerParams(dimension_semantics=sem, vmem_limit_bytes=VMEM_LIMIT)


def _resident(shape, index_map):
    return pl.BlockSpec(shape, index_map, pipeline_mode=pl.Buffered(1))


def _ln_mod(x, gain, shift, scale):
    ms = jnp.mean(x * x, axis=-1, keepdims=True)
    y = x * lax.rsqrt(ms + EPS)
    return (y * gain) * (1.0 + scale) + shift


def _dot(a, b):
    return jnp.dot(a, b, preferred_element_type=F32)


def _dot_nt(a, b):
    return lax.dot_general(a, b, (((1,), (1,)), ((), ())), preferred_element_type=F32)


def _split3(a):
    hi = a.astype(BF).astype(F32)
    r1 = a - hi
    mid = r1.astype(BF).astype(F32)
    lo = (r1 - mid).astype(BF).astype(F32)
    return hi, mid, lo


def _ada_kernel(c_ref, w_ref, b_ref, o_ref):
    c = c_ref[...]
    ca = c * jax.nn.sigmoid(c)
    o_ref[...] = _dot(ca.astype(BF), w_ref[...].astype(BF)) + b_ref[...]


def _ada_call(c, w, b):
    L, Dm, N = w.shape
    B = c.shape[0]
    tn = 1024
    return pl.pallas_call(
        _ada_kernel,
        grid=(L, N // tn),
        in_specs=[
            pl.BlockSpec((B, Dm), lambda l, n: (0, 0)),
            pl.BlockSpec((None, Dm, tn), lambda l, n: (l, 0, n)),
            pl.BlockSpec((None, 1, tn), lambda l, n: (l, 0, n)),
        ],
        out_specs=pl.BlockSpec((None, B, tn), lambda l, n: (l, 0, n)),
        out_shape=jax.ShapeDtypeStruct((L, B, N), F32),
        compiler_params=_params(("arbitrary", "arbitrary")),
        name="ada",
    )(c, w, b)


RET_BATCH = 2
RET_QK_W = RET_HEADS * RET_QK
RET_V_W = RET_HEADS * RET_V


def _ret_project_pieces(bb, x_ref, ada_ref, gain_ref, pos_ref, invf_ref, w_ref,
                        q_s, k_s, v_s, sg_s):
    half = RET_QK // 2
    ctx = {}

    def rotary(dst, base, mul):
        qk = _dot(ctx["h"], w_ref[:, base: base + RET_QK_W])
        cos, sin = ctx["cos"], ctx["sin"]
        for hh in range(RET_HEADS):
            lo = hh * RET_QK
            x1 = qk[:, lo: lo + half]
            x2 = qk[:, lo + half: lo + RET_QK]
            dst[bb, :, lo: lo + half] = ((x1 * cos - x2 * sin) * mul).astype(BF)
            dst[bb, :, lo + half: lo + RET_QK] = ((x1 * sin + x2 * cos) * mul).astype(BF)

    def p_v():
        ada = ada_ref[bb]
        x = x_ref[bb]
        xn = x * lax.rsqrt(jnp.mean(x * x, axis=-1, keepdims=True) + EPS)
        ctx["h"] = (xn * (gain_ref[...] * (1.0 + ada[1:2])) + ada[0:1]).astype(BF)
        ang = pos_ref[bb].astype(F32) * invf_ref[...]
        ctx["cos"] = jnp.cos(ang)
        ctx["sin"] = jnp.sin(ang)
        v_s[bb] = _dot(ctx["h"], w_ref[:, 2 * RET_QK_W: 2 * RET_QK_W + RET_V_W]).astype(BF)

    def p_g():
        g = _dot(ctx["h"], w_ref[:, 2 * RET_QK_W + RET_V_W:])
        sg_s[bb] = (g * jax.nn.sigmoid(g)).astype(BF)

    def p_q():
        rotary(q_s, 0, 1.0)

    def p_k():
        rotary(k_s, RET_QK_W, RET_QK ** -0.5)

    return [p_v, p_g, p_q, p_k]


def _ret_head(bb, hh, chunk_decay, q_s, k_s, v_s, sg_s, gn_ref, dec_ref, qd_ref, kd_ref,
              y_ref, state_ref):
    qs = slice(hh * RET_QK, (hh + 1) * RET_QK)
    vs = slice(hh * RET_V, (hh + 1) * RET_V)
    q = q_s[bb, :, qs]
    k = k_s[bb, :, qs]
    v = v_s[bb, :, vs]
    s = _dot_nt(q, k) * dec_ref[hh]
    intra = _dot(s.astype(BF), v)
    st = state_ref[bb, hh]
    cross = _dot(q, st.astype(BF)) * qd_ref[hh]
    kdt = (k.astype(F32) * kd_ref[hh]).T.astype(BF)
    state_ref[bb, hh] = st * chunk_decay[hh] + _dot(kdt, v)
    y = intra + cross
    ms = jnp.mean(y * y, axis=-1, keepdims=True)
    yn = ((y * lax.rsqrt(ms + EPS)) * gn_ref[:, vs]).astype(BF)
    y_ref[bb, :, vs] = sg_s[bb, :, vs] * yn


def _ret_layer_kernel(chunk_decay, x_ref, ada_ref, gain_ref, pos_ref, invf_ref, w_ref,
                      gn_ref, dec_ref, qd_ref, kd_ref, y_ref,
                      state_ref, q_s, k_s, v_s, sg_s):
    @pl.when(pl.program_id(1) == 0)
    def _():
        state_ref[...] = jnp.zeros_like(state_ref)

    proj = (x_ref, ada_ref, gain_ref, pos_ref, invf_ref, w_ref, q_s, k_s, v_s, sg_s)
    head = (chunk_decay, q_s, k_s, v_s, sg_s, gn_ref, dec_ref, qd_ref, kd_ref, y_ref, state_ref)
    for piece in _ret_project_pieces(0, *proj):
        piece()
    for bb in range(RET_BATCH):
        nxt = _ret_project_pieces(bb + 1, *proj) if bb + 1 < RET_BATCH else [None] * RET_HEADS
        for hh in range(RET_HEADS):
            if nxt[hh] is not None:
                nxt[hh]()
            _ret_head(bb, hh, *head)


def _ret_layer_call(x, ada, gain, pos, invf, w, gn, dec, qd, kd, chunk_decay):
    B, S, Dm = x.shape
    tok = lambda n: pl.BlockSpec((RET_BATCH, RET_CHUNK, n), lambda b, s: (b, s, 0))
    const = lambda shape: pl.BlockSpec(shape, lambda b, s: (0,) * len(shape))
    seq_scratch = lambda n: pltpu.VMEM((RET_BATCH, RET_CHUNK, n), BF)
    return pl.pallas_call(
        functools.partial(_ret_layer_kernel, chunk_decay),
        grid=(B // RET_BATCH, S // RET_CHUNK),
        in_specs=[
            tok(Dm),
            pl.BlockSpec((RET_BATCH, 6, Dm), lambda b, s: (b, 0, 0)),
            const((1, Dm)),
            tok(1),
            const((1, RET_QK // 2)),
            _resident((None, Dm, 2 * RET_QK_W + 2 * RET_V_W), lambda b, s: (0, 0, 0)),
            const((1, RET_V_W)),
            const((RET_HEADS, RET_CHUNK, RET_CHUNK)),
            const((RET_HEADS, RET_CHUNK, 1)),
            const((RET_HEADS, RET_CHUNK, 1)),
        ],
        out_specs=tok(RET_V_W),
        out_shape=jax.ShapeDtypeStruct((B, S, RET_V_W), BF),
        scratch_shapes=[
            pltpu.VMEM((RET_BATCH, RET_HEADS, RET_QK, RET_V), F32),
            seq_scratch(RET_QK_W), seq_scratch(RET_QK_W),
            seq_scratch(RET_V_W), seq_scratch(RET_V_W),
        ],
        compiler_params=_params(("arbitrary", "arbitrary")),
        name="ret_layer",
    )(x, ada, gain, pos, invf, w, gn, dec, qd, kd)


def _mlp_kernel(x_ref, y_ref, ada_ref, gain_ref, wo_ref, w1_ref, w2_ref, o_ref, hid_ref):
    ada = ada_ref[0]
    mod = gain_ref[...] * (1.0 + ada[4:5])
    fc = 1024
    for sub in range(MLP_SUB):
        rows = slice(sub * MLP_TM, (sub + 1) * MLP_TM)
        x1 = x_ref[0, rows, :] + ada[2:3] * _dot(y_ref[0, rows, :], wo_ref[...])
        xn = x1 * lax.rsqrt(jnp.mean(x1 * x1, axis=-1, keepdims=True) + EPS)
        h = (xn * mod + ada[3:4]).astype(BF)
        for c in range(D_FF // fc):
            a = jnp.maximum(_dot(h, w1_ref[:, c * fc:(c + 1) * fc]), 0.0)
            hid_ref[rows, c * fc:(c + 1) * fc] = (a * a).astype(BF)
        o_ref[0, rows, :] = x1 + ada[5:6] * _dot(hid_ref[rows, :], w2_ref[...])


MLP_TM = 256
MLP_SUB = 2


def _mlp_call(x, y, ada, gain, wo, w1, w2, layer):
    B, S, Dm = x.shape
    Ky = y.shape[-1]
    tm = MLP_TM * MLP_SUB
    tok = lambda n: pl.BlockSpec((1, tm, n), lambda b, s: (b, s, 0))
    return pl.pallas_call(
        _mlp_kernel,
        grid=(B, S // tm),
        in_specs=[
            tok(Dm), tok(Ky),
            pl.BlockSpec((1, 6, Dm), lambda b, s: (b, 0, 0)),
            pl.BlockSpec((1, Dm), lambda b, s: (0, 0)),
            _resident((None, Ky, Dm), lambda b, s: (0, 0, 0)),
            _resident((None, Dm, D_FF), lambda b, s: (layer, 0, 0)),
            _resident((None, D_FF, Dm), lambda b, s: (layer, 0, 0)),
        ],
        out_specs=tok(Dm),
        out_shape=jax.ShapeDtypeStruct((B, S, Dm), F32),
        scratch_shapes=[pltpu.VMEM((tm, D_FF), BF)],
        compiler_params=_params(("arbitrary", "arbitrary")),
        name=f"mlp{layer}",
    )(x, y, ada, gain, wo, w1, w2)


def _group_ones(n, group):
    r = lax.broadcasted_iota(jnp.int32, (n, n), 0) // group
    c = lax.broadcasted_iota(jnp.int32, (n, n), 1) // group
    return jnp.where(r == c, 1.0, 0.0).astype(BF)


def _head_rms(a, ones_blk):
    ss = _dot((a * a).astype(BF), ones_blk)
    return lax.rsqrt(ss * (1.0 / FOX_DH) + EPS)


def _log_sigmoid(z):
    return jnp.minimum(z, 0.0) - jnp.log(1.0 + jnp.exp(-jnp.abs(z)))


ATT_T = 256
VT_PAD = 16
VT_ROWS = FOX_DH + VT_PAD


KVQ_SUB = 2
KVQ_STEP = KVQ_SUB * ATT_T


def _kvq_kernel(x_ref, adakv_ref, ada_ref, gkv_ref, gq_ref, wkv_ref, wq_ref, wf3_ref, fb3_ref,
                kg_ref, qg_ref,
                k_ref, kb_ref, vt_ref, f2t_ref, q_ref, sog_ref, carry_row):
    @pl.when(pl.program_id(1) == 0)
    def _():
        carry_row[...] = jnp.zeros_like(carry_row)

    tm = ATT_T
    adakv = adakv_ref[0]
    ada = ada_ref[0]
    mod_kv = gkv_ref[...] * (1.0 + adakv[1:2])
    mod_q = gq_ref[...] * (1.0 + ada[1:2])
    ones_blk = _group_ones(MXU_DIM, FOX_DH)
    qscale = (FOX_DH ** -0.5) * LOG2E
    ri = lax.broadcasted_iota(jnp.int32, (tm, tm), 0)
    ci = lax.broadcasted_iota(jnp.int32, (tm, tm), 1)
    lower = jnp.where(ci <= ri, 1.0, 0.0).astype(BF)
    lane = lax.broadcasted_iota(jnp.int32, (tm, LANES), 1)
    ones_lane = jnp.where(lane < BIAS_ONE0 + 3, 1.0, 0.0)
    pad_rows = jnp.where(lax.broadcasted_iota(jnp.int32, (VT_PAD, tm), 0) == 0, 1.0, 0.0).astype(BF)

    for sub in range(KVQ_SUB):
        rows = slice(sub * tm, (sub + 1) * tm)
        x = x_ref[0, rows, :]
        xn = x * lax.rsqrt(jnp.mean(x * x, axis=-1, keepdims=True) + EPS)
        h = (xn * mod_kv + adakv[0:1]).astype(BF)
        hq = (xn * mod_q + ada[0:1]).astype(BF)

        kv = _dot(h, wkv_ref[...])
        for sl in range(D_MODEL // MXU_DIM):
            cs = slice(sl * MXU_DIM, (sl + 1) * MXU_DIM)
            a = kv[:, cs]
            k_ref[0, rows, cs] = (a * _head_rms(a, ones_blk) * kg_ref[:, cs]).astype(BF)
        for c in range(2):
            vt = kv[:, D_MODEL + c * 512: D_MODEL + (c + 1) * 512].T.astype(BF)
            for i in range(512 // FOX_DH):
                r0 = (c * (512 // FOX_DH) + i) * VT_ROWS
                vt_ref[0, sub, r0:r0 + FOX_DH, :] = vt[i * FOX_DH:(i + 1) * FOX_DH]
                vt_ref[0, sub, r0 + FOX_DH:r0 + VT_ROWS, :] = pad_rows

        hi, mid, lo = _split3(_log_sigmoid(_dot(h, wf3_ref[...]) + fb3_ref[...]))
        cum = ((_dot(lower, hi.astype(BF)) + _dot(lower, mid.astype(BF)))
               + _dot(lower, lo.astype(BF)) + carry_row[...])
        carry_row[...] = cum[tm - 1:tm, :]
        f2 = cum * LOG2E
        hi, mid, lo = _split3(f2)
        kb = jnp.where(lane < FOX_HEADS, -hi,
                       jnp.where(lane < 2 * FOX_HEADS, -mid,
                                 jnp.where(lane < BIAS_ONE0, -lo, ones_lane)))
        kb_ref[0, rows, :] = kb.astype(BF)
        f2t_ref[0, :, rows] = f2.T[:FOX_HEADS, :]

        qo = _dot(hq, wq_ref[...])
        for sl in range(D_MODEL // MXU_DIM):
            cs = slice(sl * MXU_DIM, (sl + 1) * MXU_DIM)
            a = qo[:, cs]
            q_ref[0, rows, cs] = (a * _head_rms(a, ones_blk) * (qg_ref[:, cs] * qscale)).astype(BF)
        sog_ref[0, rows, :] = jax.nn.sigmoid(qo[:, D_MODEL:]).astype(BF)


def _kvq_call(x, ada_kv, ada, gain_kv, gain_q, wkv, wq, wf3, fb3, kg, qg):
    B, S, Dm = x.shape
    ts = KVQ_STEP
    tok = lambda n: pl.BlockSpec((1, ts, n), lambda b, s: (b, s, 0))
    const = lambda shape: pl.BlockSpec(shape, lambda b, s: (0,) * len(shape))
    return pl.pallas_call(
        _kvq_kernel,
        grid=(B, S // ts),
        in_specs=[
            tok(Dm),
            pl.BlockSpec((1, 2, Dm), lambda b, s: (b, 0, 0)),
            pl.BlockSpec((1, 6, Dm), lambda b, s: (b, 0, 0)),
            const((1, Dm)), const((1, Dm)),
            _resident((Dm, 2 * Dm), lambda b, s: (0, 0)),
            _resident((None, Dm, 2 * Dm), lambda b, s: (0, 0, 0)),
            const((Dm, LANES)),
            const((1, LANES)),
            const((1, Dm)), const((1, Dm)),
        ],
        out_specs=[
            tok(Dm), tok(LANES),
            pl.BlockSpec((1, KVQ_SUB, FOX_HEADS * VT_ROWS, ATT_T), lambda b, s: (b, s, 0, 0)),
            pl.BlockSpec((1, FOX_HEADS, ts), lambda b, s: (b, 0, s)),
            tok(Dm), tok(Dm),
        ],
        out_shape=[
            jax.ShapeDtypeStruct((B, S, Dm), BF),
            jax.ShapeDtypeStruct((B, S, LANES), BF),
            jax.ShapeDtypeStruct((B, S // ATT_T, FOX_HEADS * VT_ROWS, ATT_T), BF),
            jax.ShapeDtypeStruct((B, FOX_HEADS, S), F32),
            jax.ShapeDtypeStruct((B, S, Dm), BF),
            jax.ShapeDtypeStruct((B, S, Dm), BF),
        ],
        scratch_shapes=[pltpu.VMEM((1, LANES), F32)],
        compiler_params=_params(("arbitrary", "arbitrary")),
        name="kvq",
    )(x, ada_kv, ada, gain_kv, gain_q, wkv, wq, wf3, fb3, kg, qg)


ATT_SCORE_LEAD = 1


def _attn_scores(j, hd, slot, k_ref, kb_ref, qa_ref, s_ref):
    t = ATT_T
    off = pl.multiple_of(j * t, t)
    p = hd // 2
    kcat = jnp.concatenate([k_ref[0, pl.ds(off, t), p * LANES:(p + 1) * LANES],
                            kb_ref[0, pl.ds(off, t), :]], axis=1)
    s_ref[slot, hd] = _dot(kcat, qa_ref[hd])


def _attn_update(j, hd, slot, masked, vt_ref, s_ref, m_ref, acc_ref):
    t = ATT_T
    s = s_ref[slot, hd]
    if masked:
        causal = (lax.broadcasted_iota(jnp.int32, (t, t), 0)
                  <= lax.broadcasted_iota(jnp.int32, (t, t), 1))
        s = jnp.where(causal, s, NEG_BIG)
    m = m_ref[hd:hd + 1, :]
    m_new = jnp.maximum(m, jnp.max(s, axis=0, keepdims=True))
    pr = jnp.exp2(s - m_new)
    alpha = jnp.exp2(m - m_new)
    m_ref[hd:hd + 1, :] = m_new
    vth = vt_ref[0, j, hd * VT_ROWS:(hd + 1) * VT_ROWS, :]
    acc_ref[hd] = alpha * acc_ref[hd] + _dot(vth, pr.astype(BF))


def _attn_kernel(q_ref, k_ref, kb_ref, vt_ref, f2t_ref, sog_ref, y_ref,
                 qa_ref, s_ref, m_ref, acc_ref):
    t = ATT_T
    qi = pl.program_id(1)

    @pl.when(jnp.logical_and(pl.program_id(0) == 0, qi == 0))
    def _():
        row = lax.broadcasted_iota(jnp.int32, (2 * LANES, t), 0) - LANES
        for hd in range(FOX_HEADS):
            sel = (jnp.where(row == hd, 1.0, 0.0)
                   + jnp.where(row == FOX_HEADS + hd, 1.0, 0.0)
                   + jnp.where(row == 2 * FOX_HEADS + hd, 1.0, 0.0))
            qa_ref[hd] = sel.astype(BF)

    f_row0 = LANES + BIAS_ONE0
    f_pad = jnp.zeros((VT_PAD - 3, t), F32)
    for p in range(FOX_HEADS // 2):
        cs = slice(p * LANES, (p + 1) * LANES)
        qt = q_ref[0, :, cs].T
        for hh in range(2):
            hd = 2 * p + hh
            qa_ref[hd, hh * FOX_DH:(hh + 1) * FOX_DH, :] = qt[hh * FOX_DH:(hh + 1) * FOX_DH]
            fhi, fmid, flo = _split3(f2t_ref[0, hd:hd + 1, :])
            qa_ref[hd, f_row0:f_row0 + VT_PAD, :] = jnp.concatenate(
                [fhi, fmid, flo, f_pad], axis=0).astype(BF)

    m_ref[...] = jnp.full(m_ref.shape, NEG_BIG, F32)
    acc_ref[...] = jnp.zeros_like(acc_ref)

    def stage(j, cur):
        for g in range(FOX_HEADS + ATT_SCORE_LEAD):
            if g < FOX_HEADS:
                _attn_scores(j + 1, g, 1 - cur, k_ref, kb_ref, qa_ref, s_ref)
            if g >= ATT_SCORE_LEAD:
                _attn_update(j, g - ATT_SCORE_LEAD, cur, False, vt_ref, s_ref, m_ref, acc_ref)

    def last(cur):
        for hd in range(FOX_HEADS):
            _attn_update(qi, hd, cur, True, vt_ref, s_ref, m_ref, acc_ref)

    for hd in range(FOX_HEADS):
        _attn_scores(0, hd, 0, k_ref, kb_ref, qa_ref, s_ref)

    def body(i, carry):
        stage(2 * i, 0)
        stage(2 * i + 1, 1)
        return carry

    lax.fori_loop(0, lax.shift_right_logical(qi, 1), body, 0)
    odd = lax.bitwise_and(qi, 1) == 1

    @pl.when(odd)
    def _():
        stage(qi - 1, 0)
        last(1)

    @pl.when(jnp.logical_not(odd))
    def _():
        last(0)

    for p in range(FOX_HEADS // 2):
        cs = slice(p * LANES, (p + 1) * LANES)
        halves = []
        for hd in (2 * p, 2 * p + 1):
            num = acc_ref[hd, :FOX_DH, :]
            den = acc_ref[hd, FOX_DH:FOX_DH + 1, :]
            halves.append(num * (1.0 / den))
        ot = jnp.concatenate(halves, axis=0)
        y_ref[0, :, cs] = (sog_ref[0, :, cs].astype(F32) * ot.T).astype(BF)


def _attn_call(q, k, kb, vt, f2t, sog):
    B, S, Dm = q.shape
    t = ATT_T
    tok = lambda n: pl.BlockSpec((1, t, n), lambda b, s: (b, s, 0))
    return pl.pallas_call(
        _attn_kernel,
        grid=(B, S // t),
        in_specs=[
            tok(Dm),
            pl.BlockSpec((1, S, Dm), lambda b, s: (b, 0, 0)),
            pl.BlockSpec((1, S, LANES), lambda b, s: (b, 0, 0)),
            pl.BlockSpec((1, S // t, FOX_HEADS * VT_ROWS, t), lambda b, s: (b, 0, 0, 0)),
            pl.BlockSpec((1, FOX_HEADS, t), lambda b, s: (b, 0, s)),
            tok(Dm),
        ],
        out_specs=tok(Dm),
        out_shape=jax.ShapeDtypeStruct((B, S, Dm), BF),
        scratch_shapes=[
            pltpu.VMEM((FOX_HEADS, 2 * LANES, t), BF),
            pltpu.VMEM((2, FOX_HEADS, t, t), F32),
            pltpu.VMEM((FOX_HEADS, t), F32),
            pltpu.VMEM((FOX_HEADS, VT_ROWS, t), F32),
        ],
        compiler_params=_params(("arbitrary", "arbitrary")),
        name="fox_attn",
    )(q, k, kb, vt, f2t, sog)


def _retention_constants():
    gam = 1.0 - np.power(2.0, -5.0 - np.arange(RET_HEADS, dtype=np.float64))
    log_g = np.log(gam)
    idx = np.arange(RET_CHUNK, dtype=np.float64)
    rel = idx[:, None] - idx[None, :]
    dec = np.where(rel >= 0, np.exp(log_g[:, None, None] * np.maximum(rel, 0.0)), 0.0)
    qd = np.exp(log_g[:, None] * (idx + 1.0))[:, :, None]
    kd = np.exp(log_g[:, None] * (RET_CHUNK - 1.0 - idx))[:, :, None]
    chunk_decay = tuple(float(v) for v in np.exp(log_g * RET_CHUNK))
    return (jnp.asarray(dec, F32), jnp.asarray(qd, F32), jnp.asarray(kd, F32), chunk_decay)


def kernel(x, c, positions, norm_mix_gain, norm_mlp_gain, w_ada, b_ada, w_mlp_in, w_mlp_out,
           ret_w_in, ret_norm_gain, ret_w_out, kv_norm_gain, kv_w_ada, kv_b_ada, kv_w,
           forget_bias, k_norm_gain, fox_w_in, q_norm_gain, fox_w_out):
    B, S, Dm = x.shape

    ada = _ada_call(c, w_ada, b_ada[:, None, :])
    ada = ada.reshape(2, B, 6, Dm)
    ada_kv = _ada_call(c, kv_w_ada[None], kv_b_ada[None, None, :])
    ada_kv = ada_kv.reshape(B, 2, Dm)

    w1 = w_mlp_in.astype(BF)
    w2 = w_mlp_out.astype(BF)
    rwi = ret_w_in.astype(BF)
    rwo = ret_w_out.astype(BF)
    kvw = kv_w[:, :2 * Dm].astype(BF)
    wf = kv_w[:, 2 * Dm:]
    wf3 = jnp.pad(jnp.tile(wf, (1, 3)), ((0, 0), (0, LANES - BIAS_ONE0))).astype(BF)
    fwi = fox_w_in.astype(BF)
    fwo = fox_w_out.astype(BF)

    half = RET_QK // 2
    invf = (ROPE_BASE ** (-jnp.arange(half, dtype=F32) / half))[None, :]
    dec, qd, kd, chunk_decay = _retention_constants()
    y = _ret_layer_call(x, ada[0], norm_mix_gain[0:1], positions[:, :, None], invf, rwi,
                        ret_norm_gain[0].reshape(1, RET_V_W), dec, qd, kd, chunk_decay)
    x = _mlp_call(x, y, ada[0], norm_mlp_gain[0:1], rwo, w1, w2, 0)

    fb3 = jnp.pad(jnp.tile(forget_bias, 3), (0, LANES - BIAS_ONE0))[None, :]
    kg = jnp.tile(k_norm_gain, FOX_HEADS)[None, :]
    qg = jnp.tile(q_norm_gain[0], FOX_HEADS)[None, :]
    ks, kb, vt, f2t, qf, sog = _kvq_call(x, ada_kv, ada[1], kv_norm_gain[None, :],
                                         norm_mix_gain[1:2], kvw, fwi, wf3, fb3, kg, qg)
    y = _attn_call(qf, ks, kb, vt, f2t, sog)
    x = _mlp_call(x, y, ada[1], norm_mlp_gain[1:2], fwo, w1, w2, 1)
    return x
```

```python
import functools
import math

import jax
import jax.numpy as jnp
import numpy as np
from jax import lax
from jax.experimental import pallas as pl
from jax.experimental.pallas import tpu as pltpu

F32 = jnp.float32
BF = jnp.bfloat16

D_MODEL = 1024
RET_HEADS = 4
RET_QK = 256
RET_V = 512
RET_CHUNK = 256
FOX_HEADS = 16
FOX_DH = 64
D_FF = 4096
EPS = 1e-6
ROPE_BASE = 10000.0
LOG2E = math.log2(math.e)
NEG_BIG = -1e30
LANES = 128
MXU_DIM = 256
VMEM_LIMIT = 48 * 1024 * 1024

BIAS_ONE0 = 3 * FOX_HEADS


def _params(sem):
    return pltpu.CompilerParams(dimension_semantics=sem, vmem_limit_bytes=VMEM_LIMIT)


def _resident(shape, index_map):
    return pl.BlockSpec(shape, index_map, pipeline_mode=pl.Buffered(1))


def _ln_mod(x, gain, shift, scale):
    ms = jnp.mean(x * x, axis=-1, keepdims=True)
    y = x * lax.rsqrt(ms + EPS)
    return (y * gain) * (1.0 + scale) + shift


def _dot(a, b):
    return jnp.dot(a, b, preferred_element_type=F32)


def _dot_nt(a, b):
    return lax.dot_general(a, b, (((1,), (1,)), ((), ())), preferred_element_type=F32)


def _split3(a):
    hi = a.astype(BF).astype(F32)
    r1 = a - hi
    mid = r1.astype(BF).astype(F32)
    lo = (r1 - mid).astype(BF).astype(F32)
    return hi, mid, lo


def _with_casts(body, n_in, n_out, n_cast):
    def kernel(*refs):
        ins = refs[:n_in]
        cast_in = refs[n_in:n_in + n_cast]
        outs = refs[n_in + n_cast:n_in + n_cast + n_out]
        cast_out = refs[n_in + n_cast + n_out:n_in + 2 * n_cast + n_out]
        body(*ins, *outs, *refs[n_in + 2 * n_cast + n_out:])
        for src, dst in zip(cast_in, cast_out):
            dst[...] = src[...].astype(BF)
    return kernel


def _cast_job(w3, layer, nsteps, lin, cols=None):
    n_layers, R, C = w3.shape
    cols = C if cols is None else cols
    rows = R // nsteps
    src = w3.reshape(n_layers, nsteps, rows, C)
    in_spec = pl.BlockSpec((None, None, rows, cols), lambda *g: (layer, lin(*g), 0, 0))
    out_spec = pl.BlockSpec((None, rows, cols), lambda *g: (lin(*g), 0, 0))
    return src, in_spec, out_spec, jax.ShapeDtypeStruct((nsteps, rows, cols), BF)


def _call_with_casts(body, n_in, grid, in_specs, out_specs, out_shape, jobs, args, **kw):
    nsteps = grid[0] * grid[1]
    lin = lambda b, s: b * grid[1] + s
    built = [_cast_job(w3, layer, nsteps, lin, cols) for (w3, layer, cols) in jobs]
    out_specs = list(out_specs)
    out_shape = list(out_shape)
    res = pl.pallas_call(
        _with_casts(body, n_in, len(out_specs), len(built)),
        grid=grid,
        in_specs=list(in_specs) + [b[1] for b in built],
        out_specs=out_specs + [b[2] for b in built],
        out_shape=out_shape + [b[3] for b in built],
        compiler_params=_params(("arbitrary", "arbitrary")),
        **kw,
    )(*args, *[b[0] for b in built])
    n_out = len(out_specs)
    cast = [r.reshape(r.shape[0] * r.shape[1], r.shape[2]) for r in res[n_out:]]
    return list(res[:n_out]), cast


def _ada_kernel(c_ref, w_ref, b_ref, o_ref):
    c = c_ref[...]
    ca = c * jax.nn.sigmoid(c)
    o_ref[...] = _dot(ca.astype(BF), w_ref[...].astype(BF)) + b_ref[...]


def _ada_call(c, w, b):
    L, Dm, N = w.shape
    B = c.shape[0]
    tn = 1024
    return pl.pallas_call(
        _ada_kernel,
        grid=(L, N // tn),
        in_specs=[
            pl.BlockSpec((B, Dm), lambda l, n: (0, 0)),
            pl.BlockSpec((None, Dm, tn), lambda l, n: (l, 0, n)),
            pl.BlockSpec((None, 1, tn), lambda l, n: (l, 0, n)),
        ],
        out_specs=pl.BlockSpec((None, B, tn), lambda l, n: (l, 0, n)),
        out_shape=jax.ShapeDtypeStruct((L, B, N), F32),
        compiler_params=_params(("arbitrary", "arbitrary")),
        name="ada",
    )(c, w, b)


RET_BATCH = 2
RET_QK_W = RET_HEADS * RET_QK
RET_V_W = RET_HEADS * RET_V


def _ret_project_pieces(bb, x_ref, ada_ref, gain_ref, pos_ref, invf_ref, w_ref,
                        q_s, k_s, v_s, sg_s):
    half = RET_QK // 2
    ctx = {}

    def rotary(dst, base, mul):
        qk = _dot(ctx["h"], w_ref[:, base: base + RET_QK_W])
        cos, sin = ctx["cos"], ctx["sin"]
        for hh in range(RET_HEADS):
            lo = hh * RET_QK
            x1 = qk[:, lo: lo + half]
            x2 = qk[:, lo + half: lo + RET_QK]
            dst[bb, :, lo: lo + half] = ((x1 * cos - x2 * sin) * mul).astype(BF)
            dst[bb, :, lo + half: lo + RET_QK] = ((x1 * sin + x2 * cos) * mul).astype(BF)

    def p_v():
        ada = ada_ref[bb]
        x = x_ref[bb]
        xn = x * lax.rsqrt(jnp.mean(x * x, axis=-1, keepdims=True) + EPS)
        ctx["h"] = (xn * (gain_ref[...] * (1.0 + ada[1:2])) + ada[0:1]).astype(BF)
        ang = pos_ref[bb].astype(F32) * invf_ref[...]
        ctx["cos"] = jnp.cos(ang)
        ctx["sin"] = jnp.sin(ang)
        v_s[bb] = _dot(ctx["h"], w_ref[:, 2 * RET_QK_W: 2 * RET_QK_W + RET_V_W]).astype(BF)

    def p_g():
        g = _dot(ctx["h"], w_ref[:, 2 * RET_QK_W + RET_V_W:])
        sg_s[bb] = (g * jax.nn.sigmoid(g)).astype(BF)

    def p_q():
        rotary(q_s, 0, 1.0)

    def p_k():
        rotary(k_s, RET_QK_W, RET_QK ** -0.5)

    return [p_v, p_g, p_q, p_k]


def _ret_head(bb, hh, chunk_decay, q_s, k_s, v_s, sg_s, gn_ref, dec_ref, qd_ref, kd_ref,
              y_ref, state_ref):
    qs = slice(hh * RET_QK, (hh + 1) * RET_QK)
    vs = slice(hh * RET_V, (hh + 1) * RET_V)
    q = q_s[bb, :, qs]
    k = k_s[bb, :, qs]
    v = v_s[bb, :, vs]
    s = _dot_nt(q, k) * dec_ref[hh]
    intra = _dot(s.astype(BF), v)
    st = state_ref[bb, hh]
    cross = _dot(q, st.astype(BF)) * qd_ref[hh]
    kdt = (k.astype(F32) * kd_ref[hh]).T.astype(BF)
    state_ref[bb, hh] = st * chunk_decay[hh] + _dot(kdt, v)
    y = intra + cross
    ms = jnp.mean(y * y, axis=-1, keepdims=True)
    yn = ((y * lax.rsqrt(ms + EPS)) * gn_ref[:, vs]).astype(BF)
    y_ref[bb, :, vs] = sg_s[bb, :, vs] * yn


def _ret_layer_kernel(chunk_decay, x_ref, ada_ref, gain_ref, pos_ref, invf_ref, w_ref,
                      gn_ref, dec_ref, qd_ref, kd_ref, y_ref,
                      state_ref, q_s, k_s, v_s, sg_s):
    @pl.when(pl.program_id(1) == 0)
    def _():
        state_ref[...] = jnp.zeros_like(state_ref)

    proj = (x_ref, ada_ref, gain_ref, pos_ref, invf_ref, w_ref, q_s, k_s, v_s, sg_s)
    head = (chunk_decay, q_s, k_s, v_s, sg_s, gn_ref, dec_ref, qd_ref, kd_ref, y_ref, state_ref)
    for piece in _ret_project_pieces(0, *proj):
        piece()
    for bb in range(RET_BATCH):
        nxt = _ret_project_pieces(bb + 1, *proj) if bb + 1 < RET_BATCH else [None] * RET_HEADS
        for hh in range(RET_HEADS):
            if nxt[hh] is not None:
                nxt[hh]()
            _ret_head(bb, hh, *head)


def _ret_layer_call(x, ada, gain, pos, invf, w, gn, dec, qd, kd, chunk_decay, cast_jobs):
    B, S, Dm = x.shape
    tok = lambda n: pl.BlockSpec((RET_BATCH, RET_CHUNK, n), lambda b, s: (b, s, 0))
    const = lambda shape: pl.BlockSpec(shape, lambda b, s: (0,) * len(shape))
    seq_scratch = lambda n: pltpu.VMEM((RET_BATCH, RET_CHUNK, n), BF)
    in_specs = [
        tok(Dm),
        pl.BlockSpec((RET_BATCH, 6, Dm), lambda b, s: (b, 0, 0)),
        const((1, Dm)),
        tok(1),
        const((1, RET_QK // 2)),
        _resident((None, Dm, 2 * RET_QK_W + 2 * RET_V_W), lambda b, s: (0, 0, 0)),
        const((1, RET_V_W)),
        const((RET_HEADS, RET_CHUNK, RET_CHUNK)),
        const((RET_HEADS, RET_CHUNK, 1)),
        const((RET_HEADS, RET_CHUNK, 1)),
    ]
    (y,), cast = _call_with_casts(
        functools.partial(_ret_layer_kernel, chunk_decay), len(in_specs),
        (B // RET_BATCH, S // RET_CHUNK), in_specs,
        [tok(RET_V_W)], [jax.ShapeDtypeStruct((B, S, RET_V_W), BF)], cast_jobs,
        (x, ada, gain, pos, invf, w, gn, dec, qd, kd),
        scratch_shapes=[
            pltpu.VMEM((RET_BATCH, RET_HEADS, RET_QK, RET_V), F32),
            seq_scratch(RET_QK_W), seq_scratch(RET_QK_W),
            seq_scratch(RET_V_W), seq_scratch(RET_V_W),
        ],
        name="ret_layer")
    return y, cast


def _mlp_kernel(x_ref, y_ref, ada_ref, gain_ref, wo_ref, w1_ref, w2_ref, o_ref, hid_ref):
    ada = ada_ref[0]
    mod = gain_ref[...] * (1.0 + ada[4:5])
    fc = 1024
    for sub in range(MLP_SUB):
        rows = slice(sub * MLP_TM, (sub + 1) * MLP_TM)
        x1 = x_ref[0, rows, :] + ada[2:3] * _dot(y_ref[0, rows, :], wo_ref[...])
        xn = x1 * lax.rsqrt(jnp.mean(x1 * x1, axis=-1, keepdims=True) + EPS)
        h = (xn * mod + ada[3:4]).astype(BF)
        for c in range(D_FF // fc):
            a = jnp.maximum(_dot(h, w1_ref[:, c * fc:(c + 1) * fc]), 0.0)
            hid_ref[rows, c * fc:(c + 1) * fc] = (a * a).astype(BF)
        o_ref[0, rows, :] = x1 + ada[5:6] * _dot(hid_ref[rows, :], w2_ref[...])


MLP_TM = 256
MLP_SUB = 2


def _mlp_call(x, y, ada, gain, wo, w1, w2, name, cast_jobs):
    B, S, Dm = x.shape
    Ky = y.shape[-1]
    tm = MLP_TM * MLP_SUB
    tok = lambda n: pl.BlockSpec((1, tm, n), lambda b, s: (b, s, 0))
    in_specs = [
        tok(Dm), tok(Ky),
        pl.BlockSpec((1, 6, Dm), lambda b, s: (b, 0, 0)),
        pl.BlockSpec((1, Dm), lambda b, s: (0, 0)),
        _resident((Ky, Dm), lambda b, s: (0, 0)),
        _resident((Dm, D_FF), lambda b, s: (0, 0)),
        _resident((D_FF, Dm), lambda b, s: (0, 0)),
    ]
    (out,), cast = _call_with_casts(
        _mlp_kernel, len(in_specs), (B, S // tm), in_specs,
        [tok(Dm)], [jax.ShapeDtypeStruct((B, S, Dm), F32)], cast_jobs,
        (x, y, ada, gain, wo, w1, w2),
        scratch_shapes=[pltpu.VMEM((tm, D_FF), BF)],
        name=name)
    return out, cast


def _group_ones(n, group):
    r = lax.broadcasted_iota(jnp.int32, (n, n), 0) // group
    c = lax.broadcasted_iota(jnp.int32, (n, n), 1) // group
    return jnp.where(r == c, 1.0, 0.0).astype(BF)


def _head_rms(a, ones_blk):
    ss = _dot((a * a).astype(BF), ones_blk)
    return lax.rsqrt(ss * (1.0 / FOX_DH) + EPS)


def _log_sigmoid(z):
    return jnp.minimum(z, 0.0) - jnp.log(1.0 + jnp.exp(-jnp.abs(z)))


ATT_T = 256
VT_PAD = 16
VT_ROWS = FOX_DH + VT_PAD


KVQ_SUB = 2
KVQ_STEP = KVQ_SUB * ATT_T


def _kvq_kernel(x_ref, adakv_ref, ada_ref, gkv_ref, gq_ref, wkv_ref, wq_ref, wf3_ref, fb3_ref,
                kg_ref, qg_ref,
                k_ref, kb_ref, vt_ref, f2t_ref, q_ref, sog_ref, carry_row):
    @pl.when(pl.program_id(1) == 0)
    def _():
        carry_row[...] = jnp.zeros_like(carry_row)

    tm = ATT_T
    adakv = adakv_ref[0]
    ada = ada_ref[0]
    mod_kv = gkv_ref[...] * (1.0 + adakv[1:2])
    mod_q = gq_ref[...] * (1.0 + ada[1:2])
    ones_blk = _group_ones(MXU_DIM, FOX_DH)
    qscale = (FOX_DH ** -0.5) * LOG2E
    ri = lax.broadcasted_iota(jnp.int32, (tm, tm), 0)
    ci = lax.broadcasted_iota(jnp.int32, (tm, tm), 1)
    lower = jnp.where(ci <= ri, 1.0, 0.0).astype(BF)
    lane = lax.broadcasted_iota(jnp.int32, (tm, LANES), 1)
    ones_lane = jnp.where(lane < BIAS_ONE0 + 3, 1.0, 0.0)
    pad_rows = jnp.where(lax.broadcasted_iota(jnp.int32, (VT_PAD, tm), 0) == 0, 1.0, 0.0).astype(BF)

    for sub in range(KVQ_SUB):
        rows = slice(sub * tm, (sub + 1) * tm)
        x = x_ref[0, rows, :]
        xn = x * lax.rsqrt(jnp.mean(x * x, axis=-1, keepdims=True) + EPS)
        h = (xn * mod_kv + adakv[0:1]).astype(BF)
        hq = (xn * mod_q + ada[0:1]).astype(BF)

        kv = _dot(h, wkv_ref[...])
        for sl in range(D_MODEL // MXU_DIM):
            cs = slice(sl * MXU_DIM, (sl + 1) * MXU_DIM)
            a = kv[:, cs]
            k_ref[0, rows, cs] = (a * _head_rms(a, ones_blk) * kg_ref[:, cs]).astype(BF)
        for c in range(2):
            vt = kv[:, D_MODEL + c * 512: D_MODEL + (c + 1) * 512].T.astype(BF)
            for i in range(512 // FOX_DH):
                r0 = (c * (512 // FOX_DH) + i) * VT_ROWS
                vt_ref[0, sub, r0:r0 + FOX_DH, :] = vt[i * FOX_DH:(i + 1) * FOX_DH]
                vt_ref[0, sub, r0 + FOX_DH:r0 + VT_ROWS, :] = pad_rows

        hi, mid, lo = _split3(_log_sigmoid(_dot(h, wf3_ref[...]) + fb3_ref[...]))
        cum = ((_dot(lower, hi.astype(BF)) + _dot(lower, mid.astype(BF)))
               + _dot(lower, lo.astype(BF)) + carry_row[...])
        carry_row[...] = cum[tm - 1:tm, :]
        f2 = cum * LOG2E
        hi, mid, lo = _split3(f2)
        kb = jnp.where(lane < FOX_HEADS, -hi,
                       jnp.where(lane < 2 * FOX_HEADS, -mid,
                                 jnp.where(lane < BIAS_ONE0, -lo, ones_lane)))
        kb_ref[0, rows, :] = kb.astype(BF)
        f2t_ref[0, :, rows] = f2.T[:FOX_HEADS, :]

        qo = _dot(hq, wq_ref[...])
        for sl in range(D_MODEL // MXU_DIM):
            cs = slice(sl * MXU_DIM, (sl + 1) * MXU_DIM)
            a = qo[:, cs]
            q_ref[0, rows, cs] = (a * _head_rms(a, ones_blk) * (qg_ref[:, cs] * qscale)).astype(BF)
        sog_ref[0, rows, :] = jax.nn.sigmoid(qo[:, D_MODEL:]).astype(BF)


def _kvq_call(x, ada_kv, ada, gain_kv, gain_q, wkv, wq, wf3, fb3, kg, qg, cast_jobs):
    B, S, Dm = x.shape
    ts = KVQ_STEP
    tok = lambda n: pl.BlockSpec((1, ts, n), lambda b, s: (b, s, 0))
    const = lambda shape: pl.BlockSpec(shape, lambda b, s: (0,) * len(shape))
    in_specs = [
        tok(Dm),
        pl.BlockSpec((1, 2, Dm), lambda b, s: (b, 0, 0)),
        pl.BlockSpec((1, 6, Dm), lambda b, s: (b, 0, 0)),
        const((1, Dm)), const((1, Dm)),
        _resident((Dm, 2 * Dm), lambda b, s: (0, 0)),
        _resident((Dm, 2 * Dm), lambda b, s: (0, 0)),
        const((Dm, LANES)),
        const((1, LANES)),
        const((1, Dm)), const((1, Dm)),
    ]
    out_specs = [
        tok(Dm), tok(LANES),
        pl.BlockSpec((1, KVQ_SUB, FOX_HEADS * VT_ROWS, ATT_T), lambda b, s: (b, s, 0, 0)),
        pl.BlockSpec((1, FOX_HEADS, ts), lambda b, s: (b, 0, s)),
        tok(Dm), tok(Dm),
    ]
    out_shape = [
        jax.ShapeDtypeStruct((B, S, Dm), BF),
        jax.ShapeDtypeStruct((B, S, LANES), BF),
        jax.ShapeDtypeStruct((B, S // ATT_T, FOX_HEADS * VT_ROWS, ATT_T), BF),
        jax.ShapeDtypeStruct((B, FOX_HEADS, S), F32),
        jax.ShapeDtypeStruct((B, S, Dm), BF),
        jax.ShapeDtypeStruct((B, S, Dm), BF),
    ]
    return _call_with_casts(
        _kvq_kernel, len(in_specs), (B, S // ts), in_specs, out_specs, out_shape, cast_jobs,
        (x, ada_kv, ada, gain_kv, gain_q, wkv, wq, wf3, fb3, kg, qg),
        scratch_shapes=[pltpu.VMEM((1, LANES), F32)],
        name="kvq")


ATT_SCORE_LEAD = 1


def _attn_scores(j, hd, slot, k_ref, kb_ref, qa_ref, s_ref):
    t = ATT_T
    off = pl.multiple_of(j * t, t)
    p = hd // 2
    kcat = jnp.concatenate([k_ref[0, pl.ds(off, t), p * LANES:(p + 1) * LANES],
                            kb_ref[0, pl.ds(off, t), :]], axis=1)
    s_ref[slot, hd] = _dot(kcat, qa_ref[hd])


def _attn_update(j, hd, slot, masked, vt_ref, s_ref, m_ref, acc_ref):
    t = ATT_T
    s = s_ref[slot, hd]
    if masked:
        causal = (lax.broadcasted_iota(jnp.int32, (t, t), 0)
                  <= lax.broadcasted_iota(jnp.int32, (t, t), 1))
        s = jnp.where(causal, s, NEG_BIG)
    m = m_ref[hd:hd + 1, :]
    m_new = jnp.maximum(m, jnp.max(s, axis=0, keepdims=True))
    pr = jnp.exp2(s - m_new)
    alpha = jnp.exp2(m - m_new)
    m_ref[hd:hd + 1, :] = m_new
    vth = vt_ref[0, j, hd * VT_ROWS:(hd + 1) * VT_ROWS, :]
    acc_ref[hd] = alpha * acc_ref[hd] + _dot(vth, pr.astype(BF))


def _attn_kernel(q_ref, k_ref, kb_ref, vt_ref, f2t_ref, sog_ref, y_ref,
                 qa_ref, s_ref, m_ref, acc_ref):
    t = ATT_T
    qi = pl.program_id(1)

    @pl.when(jnp.logical_and(pl.program_id(0) == 0, qi == 0))
    def _():
        row = lax.broadcasted_iota(jnp.int32, (2 * LANES, t), 0) - LANES
        for hd in range(FOX_HEADS):
            sel = (jnp.where(row == hd, 1.0, 0.0)
                   + jnp.where(row == FOX_HEADS + hd, 1.0, 0.0)
                   + jnp.where(row == 2 * FOX_HEADS + hd, 1.0, 0.0))
            qa_ref[hd] = sel.astype(BF)

    f_row0 = LANES + BIAS_ONE0
    f_pad = jnp.zeros((VT_PAD - 3, t), F32)
    for p in range(FOX_HEADS // 2):
        cs = slice(p * LANES, (p + 1) * LANES)
        qt = q_ref[0, :, cs].T
        for hh in range(2):
            hd = 2 * p + hh
            qa_ref[hd, hh * FOX_DH:(hh + 1) * FOX_DH, :] = qt[hh * FOX_DH:(hh + 1) * FOX_DH]
            fhi, fmid, flo = _split3(f2t_ref[0, hd:hd + 1, :])
            qa_ref[hd, f_row0:f_row0 + VT_PAD, :] = jnp.concatenate(
                [fhi, fmid, flo, f_pad], axis=0).astype(BF)

    m_ref[...] = jnp.full(m_ref.shape, NEG_BIG, F32)
    acc_ref[...] = jnp.zeros_like(acc_ref)

    def stage(j, cur):
        for g in range(FOX_HEADS + ATT_SCORE_LEAD):
            if g < FOX_HEADS:
                _attn_scores(j + 1, g, 1 - cur, k_ref, kb_ref, qa_ref, s_ref)
            if g >= ATT_SCORE_LEAD:
                _attn_update(j, g - ATT_SCORE_LEAD, cur, False, vt_ref, s_ref, m_ref, acc_ref)

    def last(cur):
        for hd in range(FOX_HEADS):
            _attn_update(qi, hd, cur, True, vt_ref, s_ref, m_ref, acc_ref)

    for hd in range(FOX_HEADS):
        _attn_scores(0, hd, 0, k_ref, kb_ref, qa_ref, s_ref)

    def body(i, carry):
        stage(2 * i, 0)
        stage(2 * i + 1, 1)
        return carry

    lax.fori_loop(0, lax.shift_right_logical(qi, 1), body, 0)
    odd = lax.bitwise_and(qi, 1) == 1

    @pl.when(odd)
    def _():
        stage(qi - 1, 0)
        last(1)

    @pl.when(jnp.logical_not(odd))
    def _():
        last(0)

    for p in range(FOX_HEADS // 2):
        cs = slice(p * LANES, (p + 1) * LANES)
        halves = []
        for hd in (2 * p, 2 * p + 1):
            num = acc_ref[hd, :FOX_DH, :]
            den = acc_ref[hd, FOX_DH:FOX_DH + 1, :]
            halves.append(num * (1.0 / den))
        ot = jnp.concatenate(halves, axis=0)
        y_ref[0, :, cs] = (sog_ref[0, :, cs].astype(F32) * ot.T).astype(BF)


def _attn_call(q, k, kb, vt, f2t, sog):
    B, S, Dm = q.shape
    t = ATT_T
    tok = lambda n: pl.BlockSpec((1, t, n), lambda b, s: (b, s, 0))
    return pl.pallas_call(
        _attn_kernel,
        grid=(B, S // t),
        in_specs=[
            tok(Dm),
            pl.BlockSpec((1, S, Dm), lambda b, s: (b, 0, 0)),
            pl.BlockSpec((1, S, LANES), lambda b, s: (b, 0, 0)),
            pl.BlockSpec((1, S // t, FOX_HEADS * VT_ROWS, t), lambda b, s: (b, 0, 0, 0)),
            pl.BlockSpec((1, FOX_HEADS, t), lambda b, s: (b, 0, s)),
            tok(Dm),
        ],
        out_specs=tok(Dm),
        out_shape=jax.ShapeDtypeStruct((B, S, Dm), BF),
        scratch_shapes=[
            pltpu.VMEM((FOX_HEADS, 2 * LANES, t), BF),
            pltpu.VMEM((2, FOX_HEADS, t, t), F32),
            pltpu.VMEM((FOX_HEADS, t), F32),
            pltpu.VMEM((FOX_HEADS, VT_ROWS, t), F32),
        ],
        compiler_params=_params(("arbitrary", "arbitrary")),
        name="fox_attn",
    )(q, k, kb, vt, f2t, sog)


def _retention_constants():
    gam = 1.0 - np.power(2.0, -5.0 - np.arange(RET_HEADS, dtype=np.float64))
    log_g = np.log(gam)
    idx = np.arange(RET_CHUNK, dtype=np.float64)
    rel = idx[:, None] - idx[None, :]
    dec = np.where(rel >= 0, np.exp(log_g[:, None, None] * np.maximum(rel, 0.0)), 0.0)
    qd = np.exp(log_g[:, None] * (idx + 1.0))[:, :, None]
    kd = np.exp(log_g[:, None] * (RET_CHUNK - 1.0 - idx))[:, :, None]
    chunk_decay = tuple(float(v) for v in np.exp(log_g * RET_CHUNK))
    return (jnp.asarray(dec, F32), jnp.asarray(qd, F32), jnp.asarray(kd, F32), chunk_decay)


def kernel(x, c, positions, norm_mix_gain, norm_mlp_gain, w_ada, b_ada, w_mlp_in, w_mlp_out,
           ret_w_in, ret_norm_gain, ret_w_out, kv_norm_gain, kv_w_ada, kv_b_ada, kv_w,
           forget_bias, k_norm_gain, fox_w_in, q_norm_gain, fox_w_out):
    B, S, Dm = x.shape

    ada = _ada_call(c, w_ada, b_ada[:, None, :])
    ada = ada.reshape(2, B, 6, Dm)
    ada_kv = _ada_call(c, kv_w_ada[None], kv_b_ada[None, None, :])
    ada_kv = ada_kv.reshape(B, 2, Dm)

    rwi = ret_w_in.astype(BF)
    wf = kv_w[:, 2 * Dm:]
    wf3 = jnp.pad(jnp.tile(wf, (1, 3)), ((0, 0), (0, LANES - BIAS_ONE0))).astype(BF)

    half = RET_QK // 2
    invf = (ROPE_BASE ** (-jnp.arange(half, dtype=F32) / half))[None, :]
    dec, qd, kd, chunk_decay = _retention_constants()
    y, (w1, w2, rwo) = _ret_layer_call(
        x, ada[0], norm_mix_gain[0:1], positions[:, :, None], invf, rwi,
        ret_norm_gain[0].reshape(1, RET_V_W), dec, qd, kd, chunk_decay,
        [(w_mlp_in, 0, None), (w_mlp_out, 0, None), (ret_w_out, 0, None)])
    x, (kvw, fwi) = _mlp_call(x, y, ada[0], norm_mlp_gain[0:1], rwo, w1, w2, "mlp0",
                              [(kv_w[None], 0, 2 * Dm), (fox_w_in, 0, None)])

    fb3 = jnp.pad(jnp.tile(forget_bias, 3), (0, LANES - BIAS_ONE0))[None, :]
    kg = jnp.tile(k_norm_gain, FOX_HEADS)[None, :]
    qg = jnp.tile(q_norm_gain[0], FOX_HEADS)[None, :]
    (ks, kb, vt, f2t, qf, sog), (w1, w2, fwo) = _kvq_call(
        x, ada_kv, ada[1], kv_norm_gain[None, :], norm_mix_gain[1:2], kvw, fwi, wf3, fb3, kg, qg,
        [(w_mlp_in, 1, None), (w_mlp_out, 1, None), (fox_w_out, 0, None)])
    y = _attn_call(qf, ks, kb, vt, f2t, sog)
    x, _ = _mlp_call(x, y, ada[1], norm_mlp_gain[1:2], fwo, w1, w2, "mlp1", [])
    return x
```

```python
import functools
import math

import jax
import jax.numpy as jnp
import numpy as np
from jax import lax
from jax.experimental import pallas as pl
from jax.experimental.pallas import tpu as pltpu

F32 = jnp.float32
BF = jnp.bfloat16

D_MODEL = 1024
RET_HEADS = 4
RET_QK = 256
RET_V = 512
RET_CHUNK = 256
FOX_HEADS = 16
FOX_DH = 64
D_FF = 4096
EPS = 1e-6
ROPE_BASE = 10000.0
LOG2E = math.log2(math.e)
NEG_BIG = -1e30
LANES = 128
MXU_DIM = 256
VMEM_LIMIT = 48 * 1024 * 1024

BIAS_ONE0 = 3 * FOX_HEADS


def _params(sem):
    return pltpu.CompilerParams(dimension_semantics=sem, vmem_limit_bytes=VMEM_LIMIT)


def _resident(shape, index_map):
    return pl.BlockSpec(shape, index_map, pipeline_mode=pl.Buffered(1))


def _ln_mod(x, gain, shift, scale):
    ms = jnp.mean(x * x, axis=-1, keepdims=True)
    y = x * lax.rsqrt(ms + EPS)
    return (y * gain) * (1.0 + scale) + shift


def _dot(a, b):
    return jnp.dot(a, b, preferred_element_type=F32)


def _dot_nt(a, b):
    return lax.dot_general(a, b, (((1,), (1,)), ((), ())), preferred_element_type=F32)


def _split3(a):
    hi = a.astype(BF).astype(F32)
    r1 = a - hi
    mid = r1.astype(BF).astype(F32)
    lo = (r1 - mid).astype(BF).astype(F32)
    return hi, mid, lo


def _with_casts(body, n_in, n_out, n_cast):
    def kernel(*refs):
        ins = refs[:n_in]
        cast_in = refs[n_in:n_in + n_cast]
        outs = refs[n_in + n_cast:n_in + n_cast + n_out]
        cast_out = refs[n_in + n_cast + n_out:n_in + 2 * n_cast + n_out]
        body(*ins, *outs, *refs[n_in + 2 * n_cast + n_out:])
        for src, dst in zip(cast_in, cast_out):
            dst[...] = src[...].astype(BF)
    return kernel


def _cast_job(w3, layer, nsteps, lin, cols=None):
    _, R, C = w3.shape
    cols = C if cols is None else cols
    rows = R // nsteps
    in_spec = pl.BlockSpec((None, rows, cols), lambda *g: (layer, lin(*g), 0))
    out_spec = pl.BlockSpec((rows, cols), lambda *g: (lin(*g), 0))
    return in_spec, out_spec, jax.ShapeDtypeStruct((R, cols), BF)


def _call_with_casts(body, n_in, grid, in_specs, out_specs, out_shape, jobs, args, **kw):
    nsteps = grid[0] * grid[1]
    lin = lambda b, s: b * grid[1] + s
    built = [_cast_job(w3, layer, nsteps, lin, cols) for (w3, layer, cols) in jobs]
    out_specs = list(out_specs)
    out_shape = list(out_shape)
    res = pl.pallas_call(
        _with_casts(body, n_in, len(out_specs), len(built)),
        grid=grid,
        in_specs=list(in_specs) + [b[0] for b in built],
        out_specs=out_specs + [b[1] for b in built],
        out_shape=out_shape + [b[2] for b in built],
        compiler_params=_params(("arbitrary", "arbitrary")),
        **kw,
    )(*args, *[job[0] for job in jobs])
    n_out = len(out_specs)
    return list(res[:n_out]), list(res[n_out:])


def _ada_kernel(c_ref, w_ref, b_ref, o_ref):
    c = c_ref[...]
    ca = c * jax.nn.sigmoid(c)
    o_ref[...] = _dot(ca.astype(BF), w_ref[...].astype(BF)) + b_ref[...]


def _ada_call(c, w, b):
    L, Dm, N = w.shape
    B = c.shape[0]
    tn = 1024
    return pl.pallas_call(
        _ada_kernel,
        grid=(L, N // tn),
        in_specs=[
            pl.BlockSpec((B, Dm), lambda l, n: (0, 0)),
            pl.BlockSpec((None, Dm, tn), lambda l, n: (l, 0, n)),
            pl.BlockSpec((None, 1, tn), lambda l, n: (l, 0, n)),
        ],
        out_specs=pl.BlockSpec((None, B, tn), lambda l, n: (l, 0, n)),
        out_shape=jax.ShapeDtypeStruct((L, B, N), F32),
        compiler_params=_params(("arbitrary", "arbitrary")),
        name="ada",
    )(c, w, b)


RET_BATCH = 2
RET_QK_W = RET_HEADS * RET_QK
RET_V_W = RET_HEADS * RET_V


def _ret_project_pieces(bb, x_ref, ada_ref, gain_ref, pos_ref, invf_ref, w_ref,
                        q_s, k_s, v_s, sg_s):
    half = RET_QK // 2
    ctx = {}

    def rotary(dst, base, mul):
        qk = _dot(ctx["h"], w_ref[:, base: base + RET_QK_W])
        cos, sin = ctx["cos"], ctx["sin"]
        for hh in range(RET_HEADS):
            lo = hh * RET_QK
            x1 = qk[:, lo: lo + half]
            x2 = qk[:, lo + half: lo + RET_QK]
            dst[bb, :, lo: lo + half] = ((x1 * cos - x2 * sin) * mul).astype(BF)
            dst[bb, :, lo + half: lo + RET_QK] = ((x1 * sin + x2 * cos) * mul).astype(BF)

    def p_v():
        ada = ada_ref[bb]
        x = x_ref[bb]
        xn = x * lax.rsqrt(jnp.mean(x * x, axis=-1, keepdims=True) + EPS)
        ctx["h"] = (xn * (gain_ref[...] * (1.0 + ada[1:2])) + ada[0:1]).astype(BF)
        pos_col = jnp.broadcast_to(pos_ref[bb, 0].astype(F32), (RET_QK // 2, RET_CHUNK)).T
        ang = pos_col * invf_ref[...]
        ctx["cos"] = jnp.cos(ang)
        ctx["sin"] = jnp.sin(ang)
        v_s[bb] = _dot(ctx["h"], w_ref[:, 2 * RET_QK_W: 2 * RET_QK_W + RET_V_W]).astype(BF)

    def p_g():
        g = _dot(ctx["h"], w_ref[:, 2 * RET_QK_W + RET_V_W:])
        sg_s[bb] = (g * jax.nn.sigmoid(g)).astype(BF)

    def p_q():
        rotary(q_s, 0, 1.0)

    def p_k():
        rotary(k_s, RET_QK_W, RET_QK ** -0.5)

    return [p_v, p_g, p_q, p_k]


def _ret_head(bb, hh, chunk_decay, q_s, k_s, v_s, sg_s, gn_ref, dec_ref, qd_ref, kd_ref,
              y_ref, state_ref):
    qs = slice(hh * RET_QK, (hh + 1) * RET_QK)
    vs = slice(hh * RET_V, (hh + 1) * RET_V)
    q = q_s[bb, :, qs]
    k = k_s[bb, :, qs]
    v = v_s[bb, :, vs]
    s = _dot_nt(q, k) * dec_ref[hh]
    intra = _dot(s.astype(BF), v)
    st = state_ref[bb, hh]
    cross = _dot(q, st.astype(BF)) * qd_ref[hh]
    kdt = (k.astype(F32) * kd_ref[hh]).T.astype(BF)
    state_ref[bb, hh] = st * chunk_decay[hh] + _dot(kdt, v)
    y = intra + cross
    ms = jnp.mean(y * y, axis=-1, keepdims=True)
    yn = ((y * lax.rsqrt(ms + EPS)) * gn_ref[:, vs]).astype(BF)
    y_ref[bb, :, vs] = sg_s[bb, :, vs] * yn


def _ret_layer_kernel(chunk_decay, x_ref, ada_ref, gain_ref, pos_ref, invf_ref, w_ref,
                      gn_ref, dec_ref, qd_ref, kd_ref, y_ref,
                      state_ref, q_s, k_s, v_s, sg_s):
    @pl.when(pl.program_id(1) == 0)
    def _():
        state_ref[...] = jnp.zeros_like(state_ref)

    proj = (x_ref, ada_ref, gain_ref, pos_ref, invf_ref, w_ref, q_s, k_s, v_s, sg_s)
    head = (chunk_decay, q_s, k_s, v_s, sg_s, gn_ref, dec_ref, qd_ref, kd_ref, y_ref, state_ref)
    for piece in _ret_project_pieces(0, *proj):
        piece()
    for bb in range(RET_BATCH):
        nxt = _ret_project_pieces(bb + 1, *proj) if bb + 1 < RET_BATCH else [None] * RET_HEADS
        for hh in range(RET_HEADS):
            if nxt[hh] is not None:
                nxt[hh]()
            _ret_head(bb, hh, *head)


def _ret_layer_call(x, ada, gain, pos, invf, w, gn, dec, qd, kd, chunk_decay, cast_jobs):
    B, S, Dm = x.shape
    tok = lambda n: pl.BlockSpec((RET_BATCH, RET_CHUNK, n), lambda b, s: (b, s, 0))
    const = lambda shape: pl.BlockSpec(shape, lambda b, s: (0,) * len(shape))
    seq_scratch = lambda n: pltpu.VMEM((RET_BATCH, RET_CHUNK, n), BF)
    in_specs = [
        tok(Dm),
        pl.BlockSpec((RET_BATCH, 6, Dm), lambda b, s: (b, 0, 0)),
        const((1, Dm)),
        pl.BlockSpec((RET_BATCH, 1, 1, RET_CHUNK), lambda b, s: (b, s, 0, 0)),
        const((1, RET_QK // 2)),
        _resident((None, Dm, 2 * RET_QK_W + 2 * RET_V_W), lambda b, s: (0, 0, 0)),
        const((1, RET_V_W)),
        const((RET_HEADS, RET_CHUNK, RET_CHUNK)),
        const((RET_HEADS, RET_CHUNK, 1)),
        const((RET_HEADS, RET_CHUNK, 1)),
    ]
    (y,), cast = _call_with_casts(
        functools.partial(_ret_layer_kernel, chunk_decay), len(in_specs),
        (B // RET_BATCH, S // RET_CHUNK), in_specs,
        [tok(RET_V_W)], [jax.ShapeDtypeStruct((B, S, RET_V_W), BF)], cast_jobs,
        (x, ada, gain, pos, invf, w, gn, dec, qd, kd),
        scratch_shapes=[
            pltpu.VMEM((RET_BATCH, RET_HEADS, RET_QK, RET_V), F32),
            seq_scratch(RET_QK_W), seq_scratch(RET_QK_W),
            seq_scratch(RET_V_W), seq_scratch(RET_V_W),
        ],
        name="ret_layer")
    return y, cast


def _mlp_kernel(x_ref, y_ref, ada_ref, gain_ref, wo_ref, w1_ref, w2_ref, o_ref, hid_ref):
    ada = ada_ref[0]
    mod = gain_ref[...] * (1.0 + ada[4:5])
    fc = 1024
    for sub in range(MLP_SUB):
        rows = slice(sub * MLP_TM, (sub + 1) * MLP_TM)
        x1 = x_ref[0, rows, :] + ada[2:3] * _dot(y_ref[0, rows, :], wo_ref[...])
        xn = x1 * lax.rsqrt(jnp.mean(x1 * x1, axis=-1, keepdims=True) + EPS)
        h = (xn * mod + ada[3:4]).astype(BF)
        for c in range(D_FF // fc):
            a = jnp.maximum(_dot(h, w1_ref[:, c * fc:(c + 1) * fc]), 0.0)
            hid_ref[rows, c * fc:(c + 1) * fc] = (a * a).astype(BF)
        o_ref[0, rows, :] = x1 + ada[5:6] * _dot(hid_ref[rows, :], w2_ref[...])


MLP_TM = 256
MLP_SUB = 2


def _mlp_call(x, y, ada, gain, wo, w1, w2, name, cast_jobs):
    B, S, Dm = x.shape
    Ky = y.shape[-1]
    tm = MLP_TM * MLP_SUB
    tok = lambda n: pl.BlockSpec((1, tm, n), lambda b, s: (b, s, 0))
    in_specs = [
        tok(Dm), tok(Ky),
        pl.BlockSpec((1, 6, Dm), lambda b, s: (b, 0, 0)),
        pl.BlockSpec((1, Dm), lambda b, s: (0, 0)),
        _resident((Ky, Dm), lambda b, s: (0, 0)),
        _resident((Dm, D_FF), lambda b, s: (0, 0)),
        _resident((D_FF, Dm), lambda b, s: (0, 0)),
    ]
    (out,), cast = _call_with_casts(
        _mlp_kernel, len(in_specs), (B, S // tm), in_specs,
        [tok(Dm)], [jax.ShapeDtypeStruct((B, S, Dm), F32)], cast_jobs,
        (x, y, ada, gain, wo, w1, w2),
        scratch_shapes=[pltpu.VMEM((tm, D_FF), BF)],
        name=name)
    return out, cast


def _group_ones(n, group):
    r = lax.broadcasted_iota(jnp.int32, (n, n), 0) // group
    c = lax.broadcasted_iota(jnp.int32, (n, n), 1) // group
    return jnp.where(r == c, 1.0, 0.0).astype(BF)


def _head_rms(a, ones_blk):
    ss = _dot((a * a).astype(BF), ones_blk)
    return lax.rsqrt(ss * (1.0 / FOX_DH) + EPS)


def _log_sigmoid(z):
    return jnp.minimum(z, 0.0) - jnp.log(1.0 + jnp.exp(-jnp.abs(z)))


ATT_T = 256
VT_PAD = 16
VT_ROWS = FOX_DH + VT_PAD


KVQ_SUB = 2
KVQ_STEP = KVQ_SUB * ATT_T


def _kvq_kernel(x_ref, adakv_ref, ada_ref, gkv_ref, gq_ref, wkv_ref, wq_ref, wf3_ref, fb3_ref,
                kg_ref, qg_ref,
                k_ref, kb_ref, vt_ref, f2t_ref, q_ref, sog_ref, carry_row):
    @pl.when(pl.program_id(1) == 0)
    def _():
        carry_row[...] = jnp.zeros_like(carry_row)

    tm = ATT_T
    adakv = adakv_ref[0]
    ada = ada_ref[0]
    mod_kv = gkv_ref[...] * (1.0 + adakv[1:2])
    mod_q = gq_ref[...] * (1.0 + ada[1:2])
    ones_blk = _group_ones(MXU_DIM, FOX_DH)
    qscale = (FOX_DH ** -0.5) * LOG2E
    ri = lax.broadcasted_iota(jnp.int32, (tm, tm), 0)
    ci = lax.broadcasted_iota(jnp.int32, (tm, tm), 1)
    lower = jnp.where(ci <= ri, 1.0, 0.0).astype(BF)
    lane = lax.broadcasted_iota(jnp.int32, (tm, LANES), 1)
    ones_lane = jnp.where(lane < BIAS_ONE0 + 3, 1.0, 0.0)
    pad_rows = jnp.where(lax.broadcasted_iota(jnp.int32, (VT_PAD, tm), 0) == 0, 1.0, 0.0).astype(BF)

    for sub in range(KVQ_SUB):
        rows = slice(sub * tm, (sub + 1) * tm)
        x = x_ref[0, rows, :]
        xn = x * lax.rsqrt(jnp.mean(x * x, axis=-1, keepdims=True) + EPS)
        h = (xn * mod_kv + adakv[0:1]).astype(BF)
        hq = (xn * mod_q + ada[0:1]).astype(BF)

        kv = _dot(h, wkv_ref[...])
        for sl in range(D_MODEL // MXU_DIM):
            cs = slice(sl * MXU_DIM, (sl + 1) * MXU_DIM)
            a = kv[:, cs]
            k_ref[0, rows, cs] = (a * _head_rms(a, ones_blk) * kg_ref[:, cs]).astype(BF)
        for c in range(2):
            vt = kv[:, D_MODEL + c * 512: D_MODEL + (c + 1) * 512].T.astype(BF)
            for i in range(512 // FOX_DH):
                r0 = (c * (512 // FOX_DH) + i) * VT_ROWS
                vt_ref[0, sub, r0:r0 + FOX_DH, :] = vt[i * FOX_DH:(i + 1) * FOX_DH]
                vt_ref[0, sub, r0 + FOX_DH:r0 + VT_ROWS, :] = pad_rows

        hi, mid, lo = _split3(_log_sigmoid(_dot(h, wf3_ref[...]) + fb3_ref[...]))
        cum = ((_dot(lower, hi.astype(BF)) + _dot(lower, mid.astype(BF)))
               + _dot(lower, lo.astype(BF)) + carry_row[...])
        carry_row[...] = cum[tm - 1:tm, :]
        f2 = cum * LOG2E
        hi, mid, lo = _split3(f2)
        kb = jnp.where(lane < FOX_HEADS, -hi,
                       jnp.where(lane < 2 * FOX_HEADS, -mid,
                                 jnp.where(lane < BIAS_ONE0, -lo, ones_lane)))
        kb_ref[0, rows, :] = kb.astype(BF)
        f2t_ref[0, :, rows] = f2.T[:FOX_HEADS, :]

        qo = _dot(hq, wq_ref[...])
        for sl in range(D_MODEL // MXU_DIM):
            cs = slice(sl * MXU_DIM, (sl + 1) * MXU_DIM)
            a = qo[:, cs]
            q_ref[0, rows, cs] = (a * _head_rms(a, ones_blk) * (qg_ref[:, cs] * qscale)).astype(BF)
        sog_ref[0, rows, :] = jax.nn.sigmoid(qo[:, D_MODEL:]).astype(BF)


def _kvq_call(x, ada_kv, ada, gain_kv, gain_q, wkv, wq, wf3, fb3, kg, qg, cast_jobs):
    B, S, Dm = x.shape
    ts = KVQ_STEP
    tok = lambda n: pl.BlockSpec((1, ts, n), lambda b, s: (b, s, 0))
    const = lambda shape: pl.BlockSpec(shape, lambda b, s: (0,) * len(shape))
    in_specs = [
        tok(Dm),
        pl.BlockSpec((1, 2, Dm), lambda b, s: (b, 0, 0)),
        pl.BlockSpec((1, 6, Dm), lambda b, s: (b, 0, 0)),
        const((1, Dm)), const((1, Dm)),
        _resident((Dm, 2 * Dm), lambda b, s: (0, 0)),
        _resident((Dm, 2 * Dm), lambda b, s: (0, 0)),
        const((Dm, LANES)),
        const((1, LANES)),
        const((1, Dm)), const((1, Dm)),
    ]
    out_specs = [
        tok(Dm), tok(LANES),
        pl.BlockSpec((1, KVQ_SUB, FOX_HEADS * VT_ROWS, ATT_T), lambda b, s: (b, s, 0, 0)),
        pl.BlockSpec((1, FOX_HEADS, ts), lambda b, s: (b, 0, s)),
        tok(Dm), tok(Dm),
    ]
    out_shape = [
        jax.ShapeDtypeStruct((B, S, Dm), BF),
        jax.ShapeDtypeStruct((B, S, LANES), BF),
        jax.ShapeDtypeStruct((B, S // ATT_T, FOX_HEADS * VT_ROWS, ATT_T), BF),
        jax.ShapeDtypeStruct((B, FOX_HEADS, S), F32),
        jax.ShapeDtypeStruct((B, S, Dm), BF),
        jax.ShapeDtypeStruct((B, S, Dm), BF),
    ]
    return _call_with_casts(
        _kvq_kernel, len(in_specs), (B, S // ts), in_specs, out_specs, out_shape, cast_jobs,
        (x, ada_kv, ada, gain_kv, gain_q, wkv, wq, wf3, fb3, kg, qg),
        scratch_shapes=[pltpu.VMEM((1, LANES), F32)],
        name="kvq")


ATT_SCORE_LEAD = 1


def _attn_scores(j, hd, slot, k_ref, kb_ref, qa_ref, s_ref):
    t = ATT_T
    off = pl.multiple_of(j * t, t)
    p = hd // 2
    kcat = jnp.concatenate([k_ref[0, pl.ds(off, t), p * LANES:(p + 1) * LANES],
                            kb_ref[0, pl.ds(off, t), :]], axis=1)
    s_ref[slot, hd] = _dot(kcat, qa_ref[hd])


def _attn_update(j, hd, slot, masked, vt_ref, s_ref, m_ref, acc_ref):
    t = ATT_T
    s = s_ref[slot, hd]
    if masked:
        causal = (lax.broadcasted_iota(jnp.int32, (t, t), 0)
                  <= lax.broadcasted_iota(jnp.int32, (t, t), 1))
        s = jnp.where(causal, s, NEG_BIG)
    m = m_ref[hd:hd + 1, :]
    m_new = jnp.maximum(m, jnp.max(s, axis=0, keepdims=True))
    pr = jnp.exp2(s - m_new)
    alpha = jnp.exp2(m - m_new)
    m_ref[hd:hd + 1, :] = m_new
    vth = vt_ref[0, j, hd * VT_ROWS:(hd + 1) * VT_ROWS, :]
    acc_ref[hd] = alpha * acc_ref[hd] + _dot(vth, pr.astype(BF))


def _attn_kernel(q_ref, k_ref, kb_ref, vt_ref, f2t_ref, sog_ref, y_ref,
                 qa_ref, s_ref, m_ref, acc_ref):
    t = ATT_T
    qi = pl.program_id(1)

    @pl.when(jnp.logical_and(pl.program_id(0) == 0, qi == 0))
    def _():
        row = lax.broadcasted_iota(jnp.int32, (2 * LANES, t), 0) - LANES
        for hd in range(FOX_HEADS):
            sel = (jnp.where(row == hd, 1.0, 0.0)
                   + jnp.where(row == FOX_HEADS + hd, 1.0, 0.0)
                   + jnp.where(row == 2 * FOX_HEADS + hd, 1.0, 0.0))
            qa_ref[hd] = sel.astype(BF)

    f_row0 = LANES + BIAS_ONE0
    f_pad = jnp.zeros((VT_PAD - 3, t), F32)
    for p in range(FOX_HEADS // 2):
        cs = slice(p * LANES, (p + 1) * LANES)
        qt = q_ref[0, :, cs].T
        for hh in range(2):
            hd = 2 * p + hh
            qa_ref[hd, hh * FOX_DH:(hh + 1) * FOX_DH, :] = qt[hh * FOX_DH:(hh + 1) * FOX_DH]
            fhi, fmid, flo = _split3(f2t_ref[0, hd:hd + 1, :])
            qa_ref[hd, f_row0:f_row0 + VT_PAD, :] = jnp.concatenate(
                [fhi, fmid, flo, f_pad], axis=0).astype(BF)
            _attn_scores(0, hd, 0, k_ref, kb_ref, qa_ref, s_ref)

    m_ref[...] = jnp.full(m_ref.shape, NEG_BIG, F32)
    acc_ref[...] = jnp.zeros_like(acc_ref)

    def stage(j, cur):
        for g in range(FOX_HEADS + ATT_SCORE_LEAD):
            if g < FOX_HEADS:
                _attn_scores(j + 1, g, 1 - cur, k_ref, kb_ref, qa_ref, s_ref)
            if g >= ATT_SCORE_LEAD:
                _attn_update(j, g - ATT_SCORE_LEAD, cur, False, vt_ref, s_ref, m_ref, acc_ref)

    def last(cur):
        for p in range(FOX_HEADS // 2):
            halves = []
            for hd in (2 * p, 2 * p + 1):
                _attn_update(qi, hd, cur, True, vt_ref, s_ref, m_ref, acc_ref)
                num = acc_ref[hd, :FOX_DH, :]
                den = acc_ref[hd, FOX_DH:FOX_DH + 1, :]
                halves.append(num * (1.0 / den))
            cs = slice(p * LANES, (p + 1) * LANES)
            ot = jnp.concatenate(halves, axis=0)
            y_ref[0, :, cs] = (sog_ref[0, :, cs].astype(F32) * ot.T).astype(BF)

    def body(i, carry):
        stage(2 * i, 0)
        stage(2 * i + 1, 1)
        return carry

    lax.fori_loop(0, lax.shift_right_logical(qi, 1), body, 0)
    odd = lax.bitwise_and(qi, 1) == 1

    @pl.when(odd)
    def _():
        stage(qi - 1, 0)
        last(1)

    @pl.when(jnp.logical_not(odd))
    def _():
        last(0)


def _attn_call(q, k, kb, vt, f2t, sog):
    B, S, Dm = q.shape
    t = ATT_T
    tok = lambda n: pl.BlockSpec((1, t, n), lambda b, s: (b, s, 0))
    return pl.pallas_call(
        _attn_kernel,
        grid=(B, S // t),
        in_specs=[
            tok(Dm),
            pl.BlockSpec((1, S, Dm), lambda b, s: (b, 0, 0)),
            pl.BlockSpec((1, S, LANES), lambda b, s: (b, 0, 0)),
            pl.BlockSpec((1, S // t, FOX_HEADS * VT_ROWS, t), lambda b, s: (b, 0, 0, 0)),
            pl.BlockSpec((1, FOX_HEADS, t), lambda b, s: (b, 0, s)),
            tok(Dm),
        ],
        out_specs=tok(Dm),
        out_shape=jax.ShapeDtypeStruct((B, S, Dm), BF),
        scratch_shapes=[
            pltpu.VMEM((FOX_HEADS, 2 * LANES, t), BF),
            pltpu.VMEM((2, FOX_HEADS, t, t), F32),
            pltpu.VMEM((FOX_HEADS, t), F32),
            pltpu.VMEM((FOX_HEADS, VT_ROWS, t), F32),
        ],
        compiler_params=_params(("arbitrary", "arbitrary")),
        name="fox_attn",
    )(q, k, kb, vt, f2t, sog)


def _retention_constants():
    gam = 1.0 - np.power(2.0, -5.0 - np.arange(RET_HEADS, dtype=np.float64))
    log_g = np.log(gam)
    idx = np.arange(RET_CHUNK, dtype=np.float64)
    rel = idx[:, None] - idx[None, :]
    dec = np.where(rel >= 0, np.exp(log_g[:, None, None] * np.maximum(rel, 0.0)), 0.0)
    qd = np.exp(log_g[:, None] * (idx + 1.0))[:, :, None]
    kd = np.exp(log_g[:, None] * (RET_CHUNK - 1.0 - idx))[:, :, None]
    chunk_decay = tuple(float(v) for v in np.exp(log_g * RET_CHUNK))
    return (jnp.asarray(dec, F32), jnp.asarray(qd, F32), jnp.asarray(kd, F32), chunk_decay)


def kernel(x, c, positions, norm_mix_gain, norm_mlp_gain, w_ada, b_ada, w_mlp_in, w_mlp_out,
           ret_w_in, ret_norm_gain, ret_w_out, kv_norm_gain, kv_w_ada, kv_b_ada, kv_w,
           forget_bias, k_norm_gain, fox_w_in, q_norm_gain, fox_w_out):
    B, S, Dm = x.shape

    ada = _ada_call(c, w_ada, b_ada[:, None, :])
    ada = ada.reshape(2, B, 6, Dm)
    ada_kv = _ada_call(c, kv_w_ada[None], kv_b_ada[None, None, :])
    ada_kv = ada_kv.reshape(B, 2, Dm)

    rwi = ret_w_in.astype(BF)
    wf = kv_w[:, 2 * Dm:]
    wf3 = jnp.pad(jnp.tile(wf, (1, 3)), ((0, 0), (0, LANES - BIAS_ONE0))).astype(BF)

    half = RET_QK // 2
    invf = (ROPE_BASE ** (-jnp.arange(half, dtype=F32) / half))[None, :]
    dec, qd, kd, chunk_decay = _retention_constants()
    y, (w1, w2, rwo) = _ret_layer_call(
        x, ada[0], norm_mix_gain[0:1], positions.reshape(B, S // RET_CHUNK, 1, RET_CHUNK), invf, rwi,
        ret_norm_gain[0].reshape(1, RET_V_W), dec, qd, kd, chunk_decay,
        [(w_mlp_in, 0, None), (w_mlp_out, 0, None), (ret_w_out, 0, None)])
    x, (kvw, fwi) = _mlp_call(x, y, ada[0], norm_mlp_gain[0:1], rwo, w1, w2, "mlp0",
                              [(kv_w[None], 0, 2 * Dm), (fox_w_in, 0, None)])

    fb3 = jnp.pad(jnp.tile(forget_bias, 3), (0, LANES - BIAS_ONE0))[None, :]
    kg = jnp.tile(k_norm_gain, FOX_HEADS)[None, :]
    qg = jnp.tile(q_norm_gain[0], FOX_HEADS)[None, :]
    (ks, kb, vt, f2t, qf, sog), (w1, w2, fwo) = _kvq_call(
        x, ada_kv, ada[1], kv_norm_gain[None, :], norm_mix_gain[1:2], kvw, fwi, wf3, fb3, kg, qg,
        [(w_mlp_in, 1, None), (w_mlp_out, 1, None), (fox_w_out, 0, None)])
    y = _attn_call(qf, ks, kb, vt, f2t, sog)
    x, _ = _mlp_call(x, y, ada[1], norm_mlp_gain[1:2], fwo, w1, w2, "mlp1", [])
    return x
```

```python
import functools
import math

import jax
import jax.numpy as jnp
import numpy as np
from jax import lax
from jax.experimental import pallas as pl
from jax.experimental.pallas import tpu as pltpu

F32 = jnp.float32
BF = jnp.bfloat16

D_MODEL = 1024
RET_HEADS = 4
RET_QK = 256
RET_V = 512
RET_CHUNK = 256
FOX_HEADS = 16
FOX_DH = 64
D_FF = 4096
EPS = 1e-6
ROPE_BASE = 10000.0
LOG2E = math.log2(math.e)
NEG_BIG = -1e30
LANES = 128
MXU_DIM = 256
VMEM_LIMIT = 48 * 1024 * 1024

BIAS_ONE0 = 3 * FOX_HEADS


def _params(sem):
    return pltpu.CompilerParams(dimension_semantics=sem, vmem_limit_bytes=VMEM_LIMIT)


def _resident(shape, index_map):
    return pl.BlockSpec(shape, index_map, pipeline_mode=pl.Buffered(1))


def _ln_mod(x, gain, shift, scale):
    ms = jnp.mean(x * x, axis=-1, keepdims=True)
    y = x * lax.rsqrt(ms + EPS)
    return (y * gain) * (1.0 + scale) + shift


def _dot(a, b):
    return jnp.dot(a, b, preferred_element_type=F32)


def _dot_nt(a, b):
    return lax.dot_general(a, b, (((1,), (1,)), ((), ())), preferred_element_type=F32)


def _split3(a):
    hi = a.astype(BF).astype(F32)
    r1 = a - hi
    mid = r1.astype(BF).astype(F32)
    lo = (r1 - mid).astype(BF).astype(F32)
    return hi, mid, lo


def _with_casts(body, n_in, n_out, n_cast):
    def kernel(*refs):
        ins = refs[:n_in]
        cast_in = refs[n_in:n_in + n_cast]
        outs = refs[n_in + n_cast:n_in + n_cast + n_out]
        cast_out = refs[n_in + n_cast + n_out:n_in + 2 * n_cast + n_out]
        body(*ins, *outs, *refs[n_in + 2 * n_cast + n_out:])
        for src, dst in zip(cast_in, cast_out):
            dst[...] = src[...].astype(BF)
    return kernel


def _cast_job(w3, layer, nsteps, lin, cols=None):
    R, C = w3.shape[-2:]
    cols = C if cols is None else cols
    rows = R // nsteps
    if layer is None:
        in_spec = pl.BlockSpec((rows, cols), lambda *g: (lin(*g), 0))
    else:
        in_spec = pl.BlockSpec((None, rows, cols), lambda *g: (layer, lin(*g), 0))
    out_spec = pl.BlockSpec((rows, cols), lambda *g: (lin(*g), 0))
    return in_spec, out_spec, jax.ShapeDtypeStruct((R, cols), BF)


def _call_with_casts(body, n_in, grid, in_specs, out_specs, out_shape, jobs, args, **kw):
    nsteps = grid[0] * grid[1]
    lin = lambda b, s: b * grid[1] + s
    built = [_cast_job(w3, layer, nsteps, lin, cols) for (w3, layer, cols) in jobs]
    out_specs = list(out_specs)
    out_shape = list(out_shape)
    res = pl.pallas_call(
        _with_casts(body, n_in, len(out_specs), len(built)),
        grid=grid,
        in_specs=list(in_specs) + [b[0] for b in built],
        out_specs=out_specs + [b[1] for b in built],
        out_shape=out_shape + [b[2] for b in built],
        compiler_params=_params(("arbitrary", "arbitrary")),
        **kw,
    )(*args, *[job[0] for job in jobs])
    n_out = len(out_specs)
    return list(res[:n_out]), list(res[n_out:])


def _ada_kernel(c_ref, w_ref, b_ref, o_ref):
    c = c_ref[...]
    ca = c * jax.nn.sigmoid(c)
    o_ref[...] = _dot(ca.astype(BF), w_ref[...].astype(BF)) + b_ref[...]


def _ada_call(c, w, b):
    L, Dm, N = w.shape
    B = c.shape[0]
    tn = 1024
    return pl.pallas_call(
        _ada_kernel,
        grid=(L, N // tn),
        in_specs=[
            pl.BlockSpec((B, Dm), lambda l, n: (0, 0)),
            pl.BlockSpec((None, Dm, tn), lambda l, n: (l, 0, n)),
            pl.BlockSpec((None, 1, tn), lambda l, n: (l, 0, n)),
        ],
        out_specs=pl.BlockSpec((None, B, tn), lambda l, n: (l, 0, n)),
        out_shape=jax.ShapeDtypeStruct((L, B, N), F32),
        compiler_params=_params(("arbitrary", "arbitrary")),
        name="ada",
    )(c, w, b)


RET_BATCH = 2
RET_STEP_CHUNKS = 1
RET_QK_W = RET_HEADS * RET_QK
RET_V_W = RET_HEADS * RET_V


def _ret_project_pieces(bb, c, x_ref, ada_ref, gain_ref, pos_ref, invf_ref, w_ref,
                        q_s, k_s, v_s, sg_s):
    rows = slice(c * RET_CHUNK, (c + 1) * RET_CHUNK)
    half = RET_QK // 2
    ctx = {}

    def rotary(dst, base, mul):
        qk = _dot(ctx["h"], w_ref[:, base: base + RET_QK_W])
        cos, sin = ctx["cos"], ctx["sin"]
        for hh in range(RET_HEADS):
            lo = hh * RET_QK
            x1 = qk[:, lo: lo + half]
            x2 = qk[:, lo + half: lo + RET_QK]
            dst[bb, :, lo: lo + half] = ((x1 * cos - x2 * sin) * mul).astype(BF)
            dst[bb, :, lo + half: lo + RET_QK] = ((x1 * sin + x2 * cos) * mul).astype(BF)

    def p_v():
        ada = ada_ref[bb]
        x = x_ref[bb, rows, :]
        xn = x * lax.rsqrt(jnp.mean(x * x, axis=-1, keepdims=True) + EPS)
        ctx["h"] = (xn * (gain_ref[...] * (1.0 + ada[1:2])) + ada[0:1]).astype(BF)
        pos_col = jnp.broadcast_to(pos_ref[bb, c].astype(F32), (RET_QK // 2, RET_CHUNK)).T
        ang = pos_col * invf_ref[...]
        ctx["cos"] = jnp.cos(ang)
        ctx["sin"] = jnp.sin(ang)
        v_s[bb] = _dot(ctx["h"], w_ref[:, 2 * RET_QK_W: 2 * RET_QK_W + RET_V_W]).astype(BF)

    def p_g():
        g = _dot(ctx["h"], w_ref[:, 2 * RET_QK_W + RET_V_W:])
        sg_s[bb] = (g * jax.nn.sigmoid(g)).astype(BF)

    def p_q():
        rotary(q_s, 0, 1.0)

    def p_k():
        rotary(k_s, RET_QK_W, RET_QK ** -0.5)

    return [p_v, p_g, p_q, p_k]


def _ret_head(bb, c, hh, chunk_decay, q_s, k_s, v_s, sg_s, gn_ref, dec_ref, qd_ref, kd_ref,
              y_ref, state_ref):
    rows = slice(c * RET_CHUNK, (c + 1) * RET_CHUNK)
    qs = slice(hh * RET_QK, (hh + 1) * RET_QK)
    vs = slice(hh * RET_V, (hh + 1) * RET_V)
    q = q_s[bb, :, qs]
    k = k_s[bb, :, qs]
    v = v_s[bb, :, vs]
    s = _dot_nt(q, k) * dec_ref[hh]
    intra = _dot(s.astype(BF), v)
    st = state_ref[bb, hh]
    cross = _dot(q, st.astype(BF)) * qd_ref[hh]
    kdt = (k.astype(F32) * kd_ref[hh]).T.astype(BF)
    state_ref[bb, hh] = st * chunk_decay[hh] + _dot(kdt, v)
    y = intra + cross
    ms = jnp.mean(y * y, axis=-1, keepdims=True)
    yn = ((y * lax.rsqrt(ms + EPS)) * gn_ref[:, vs]).astype(BF)
    y_ref[bb, rows, vs] = sg_s[bb, :, vs] * yn


def _ret_layer_kernel(chunk_decay, x_ref, ada_ref, gain_ref, pos_ref, invf_ref, w_ref,
                      gn_ref, dec_ref, qd_ref, kd_ref, y_ref,
                      state_ref, q_s, k_s, v_s, sg_s):
    @pl.when(pl.program_id(1) == 0)
    def _():
        state_ref[...] = jnp.zeros_like(state_ref)

    proj = (x_ref, ada_ref, gain_ref, pos_ref, invf_ref, w_ref, q_s, k_s, v_s, sg_s)
    head = (chunk_decay, q_s, k_s, v_s, sg_s, gn_ref, dec_ref, qd_ref, kd_ref, y_ref, state_ref)
    tiles = [(bb, c) for c in range(RET_STEP_CHUNKS) for bb in range(RET_BATCH)]
    for piece in _ret_project_pieces(*tiles[0], *proj):
        piece()
    for i, (bb, c) in enumerate(tiles):
        nxt = (_ret_project_pieces(*tiles[i + 1], *proj) if i + 1 < len(tiles)
               else [None] * RET_HEADS)
        for hh in range(RET_HEADS):
            if nxt[hh] is not None:
                nxt[hh]()
            _ret_head(bb, c, hh, *head)


def _ret_layer_call(x, ada, gain, pos, invf, w, gn, dec, qd, kd, chunk_decay, cast_jobs):
    B, S, Dm = x.shape
    ts = RET_STEP_CHUNKS * RET_CHUNK
    tok = lambda n: pl.BlockSpec((RET_BATCH, ts, n), lambda b, s: (b, s, 0))
    const = lambda shape: pl.BlockSpec(shape, lambda b, s: (0,) * len(shape))
    seq_scratch = lambda n: pltpu.VMEM((RET_BATCH, RET_CHUNK, n), BF)
    in_specs = [
        tok(Dm),
        pl.BlockSpec((RET_BATCH, 6, Dm), lambda b, s: (b, 0, 0)),
        const((1, Dm)),
        pl.BlockSpec((RET_BATCH, RET_STEP_CHUNKS, 1, RET_CHUNK), lambda b, s: (b, s, 0, 0)),
        const((1, RET_QK // 2)),
        _resident((None, Dm, 2 * RET_QK_W + 2 * RET_V_W), lambda b, s: (0, 0, 0)),
        const((1, RET_V_W)),
        const((RET_HEADS, RET_CHUNK, RET_CHUNK)),
        const((RET_HEADS, RET_CHUNK, 1)),
        const((RET_HEADS, RET_CHUNK, 1)),
    ]
    (y,), cast = _call_with_casts(
        functools.partial(_ret_layer_kernel, chunk_decay), len(in_specs),
        (B // RET_BATCH, S // ts), in_specs,
        [tok(RET_V_W)], [jax.ShapeDtypeStruct((B, S, RET_V_W), BF)], cast_jobs,
        (x, ada, gain, pos, invf, w, gn, dec, qd, kd),
        scratch_shapes=[
            pltpu.VMEM((RET_BATCH, RET_HEADS, RET_QK, RET_V), F32),
            seq_scratch(RET_QK_W), seq_scratch(RET_QK_W),
            seq_scratch(RET_V_W), seq_scratch(RET_V_W),
        ],
        name="ret_layer")
    return y, cast


def _mlp_kernel(x_ref, y_ref, ada_ref, gain_ref, wo_ref, w1_ref, w2_ref, o_ref, hid_ref):
    ada = ada_ref[0]
    mod = gain_ref[...] * (1.0 + ada[4:5])
    fc = 1024
    for sub in range(MLP_SUB):
        rows = slice(sub * MLP_TM, (sub + 1) * MLP_TM)
        x1 = x_ref[0, rows, :] + ada[2:3] * _dot(y_ref[0, rows, :], wo_ref[...])
        xn = x1 * lax.rsqrt(jnp.mean(x1 * x1, axis=-1, keepdims=True) + EPS)
        h = (xn * mod + ada[3:4]).astype(BF)
        for c in range(D_FF // fc):
            a = jnp.maximum(_dot(h, w1_ref[:, c * fc:(c + 1) * fc]), 0.0)
            hid_ref[rows, c * fc:(c + 1) * fc] = (a * a).astype(BF)
        o_ref[0, rows, :] = x1 + ada[5:6] * _dot(hid_ref[rows, :], w2_ref[...])


MLP_TM = 256
MLP_SUB = 2


def _mlp_call(x, y, ada, gain, wo, w1, w2, name, cast_jobs):
    B, S, Dm = x.shape
    Ky = y.shape[-1]
    tm = MLP_TM * MLP_SUB
    tok = lambda n: pl.BlockSpec((1, tm, n), lambda b, s: (b, s, 0))
    in_specs = [
        tok(Dm), tok(Ky),
        pl.BlockSpec((1, 6, Dm), lambda b, s: (b, 0, 0)),
        pl.BlockSpec((1, Dm), lambda b, s: (0, 0)),
        _resident((Ky, Dm), lambda b, s: (0, 0)),
        _resident((Dm, D_FF), lambda b, s: (0, 0)),
        _resident((D_FF, Dm), lambda b, s: (0, 0)),
    ]
    (out,), cast = _call_with_casts(
        _mlp_kernel, len(in_specs), (B, S // tm), in_specs,
        [tok(Dm)], [jax.ShapeDtypeStruct((B, S, Dm), F32)], cast_jobs,
        (x, y, ada, gain, wo, w1, w2),
        scratch_shapes=[pltpu.VMEM((tm, D_FF), BF)],
        name=name)
    return out, cast


def _group_ones(n, group):
    r = lax.broadcasted_iota(jnp.int32, (n, n), 0) // group
    c = lax.broadcasted_iota(jnp.int32, (n, n), 1) // group
    return jnp.where(r == c, 1.0, 0.0).astype(BF)


def _head_rms(a, ones_blk):
    ss = _dot((a * a).astype(BF), ones_blk)
    return lax.rsqrt(ss * (1.0 / FOX_DH) + EPS)


def _log_sigmoid(z):
    return jnp.minimum(z, 0.0) - jnp.log(1.0 + jnp.exp(-jnp.abs(z)))


ATT_T = 256
VT_PAD = 16
VT_ROWS = FOX_DH + VT_PAD


KVQ_SUB = 2
KVQ_STEP = KVQ_SUB * ATT_T


def _kvq_kernel(x_ref, adakv_ref, ada_ref, gkv_ref, gq_ref, wkv_ref, wq_ref, wf3_ref, fb3_ref,
                kg_ref, qg_ref,
                k_ref, kb_ref, vt_ref, f2t_ref, q_ref, sog_ref, carry_row):
    @pl.when(pl.program_id(1) == 0)
    def _():
        carry_row[...] = jnp.zeros_like(carry_row)

    tm = ATT_T
    adakv = adakv_ref[0]
    ada = ada_ref[0]
    mod_kv = gkv_ref[...] * (1.0 + adakv[1:2])
    mod_q = gq_ref[...] * (1.0 + ada[1:2])
    ones_blk = _group_ones(MXU_DIM, FOX_DH)
    qscale = (FOX_DH ** -0.5) * LOG2E
    ri = lax.broadcasted_iota(jnp.int32, (tm, tm), 0)
    ci = lax.broadcasted_iota(jnp.int32, (tm, tm), 1)
    lower = jnp.where(ci <= ri, 1.0, 0.0).astype(BF)
    lane = lax.broadcasted_iota(jnp.int32, (tm, LANES), 1)
    ones_lane = jnp.where(lane < BIAS_ONE0 + 3, 1.0, 0.0)
    pad_rows = jnp.where(lax.broadcasted_iota(jnp.int32, (VT_PAD, tm), 0) == 0, 1.0, 0.0).astype(BF)

    for sub in range(KVQ_SUB):
        rows = slice(sub * tm, (sub + 1) * tm)
        x = x_ref[0, rows, :]
        xn = x * lax.rsqrt(jnp.mean(x * x, axis=-1, keepdims=True) + EPS)
        h = (xn * mod_kv + adakv[0:1]).astype(BF)
        hq = (xn * mod_q + ada[0:1]).astype(BF)

        kv = _dot(h, wkv_ref[...])
        for sl in range(D_MODEL // MXU_DIM):
            cs = slice(sl * MXU_DIM, (sl + 1) * MXU_DIM)
            a = kv[:, cs]
            k_ref[0, rows, cs] = (a * _head_rms(a, ones_blk) * kg_ref[:, cs]).astype(BF)
        for c in range(2):
            vt = kv[:, D_MODEL + c * 512: D_MODEL + (c + 1) * 512].T.astype(BF)
            for i in range(512 // FOX_DH):
                r0 = (c * (512 // FOX_DH) + i) * VT_ROWS
                vt_ref[0, sub, r0:r0 + FOX_DH, :] = vt[i * FOX_DH:(i + 1) * FOX_DH]
                vt_ref[0, sub, r0 + FOX_DH:r0 + VT_ROWS, :] = pad_rows

        hi, mid, lo = _split3(_log_sigmoid(_dot(h, wf3_ref[...]) + fb3_ref[...]))
        cum = ((_dot(lower, hi.astype(BF)) + _dot(lower, mid.astype(BF)))
               + _dot(lower, lo.astype(BF)) + carry_row[...])
        carry_row[...] = cum[tm - 1:tm, :]
        f2 = cum * LOG2E
        hi, mid, lo = _split3(f2)
        kb = jnp.where(lane < FOX_HEADS, -hi,
                       jnp.where(lane < 2 * FOX_HEADS, -mid,
                                 jnp.where(lane < BIAS_ONE0, -lo, ones_lane)))
        kb_ref[0, rows, :] = kb.astype(BF)
        f2t_ref[0, :, rows] = f2.T[:FOX_HEADS, :]

        qo = _dot(hq, wq_ref[...])
        for sl in range(D_MODEL // MXU_DIM):
            cs = slice(sl * MXU_DIM, (sl + 1) * MXU_DIM)
            a = qo[:, cs]
            q_ref[0, rows, cs] = (a * _head_rms(a, ones_blk) * (qg_ref[:, cs] * qscale)).astype(BF)
        sog_ref[0, rows, :] = jax.nn.sigmoid(qo[:, D_MODEL:]).astype(BF)


def _kvq_call(x, ada_kv, ada, gain_kv, gain_q, wkv, wq, wf3, fb3, kg, qg, cast_jobs):
    B, S, Dm = x.shape
    ts = KVQ_STEP
    tok = lambda n: pl.BlockSpec((1, ts, n), lambda b, s: (b, s, 0))
    const = lambda shape: pl.BlockSpec(shape, lambda b, s: (0,) * len(shape))
    in_specs = [
        tok(Dm),
        pl.BlockSpec((1, 2, Dm), lambda b, s: (b, 0, 0)),
        pl.BlockSpec((1, 6, Dm), lambda b, s: (b, 0, 0)),
        const((1, Dm)), const((1, Dm)),
        _resident((Dm, 2 * Dm), lambda b, s: (0, 0)),
        _resident((Dm, 2 * Dm), lambda b, s: (0, 0)),
        const((Dm, LANES)),
        const((1, LANES)),
        const((1, Dm)), const((1, Dm)),
    ]
    out_specs = [
        tok(Dm), tok(LANES),
        pl.BlockSpec((1, KVQ_SUB, FOX_HEADS * VT_ROWS, ATT_T), lambda b, s: (b, s, 0, 0)),
        pl.BlockSpec((1, FOX_HEADS, ts), lambda b, s: (b, 0, s)),
        tok(Dm), tok(Dm),
    ]
    out_shape = [
        jax.ShapeDtypeStruct((B, S, Dm), BF),
        jax.ShapeDtypeStruct((B, S, LANES), BF),
        jax.ShapeDtypeStruct((B, S // ATT_T, FOX_HEADS * VT_ROWS, ATT_T), BF),
        jax.ShapeDtypeStruct((B, FOX_HEADS, S), F32),
        jax.ShapeDtypeStruct((B, S, Dm), BF),
        jax.ShapeDtypeStruct((B, S, Dm), BF),
    ]
    return _call_with_casts(
        _kvq_kernel, len(in_specs), (B, S // ts), in_specs, out_specs, out_shape, cast_jobs,
        (x, ada_kv, ada, gain_kv, gain_q, wkv, wq, wf3, fb3, kg, qg),
        scratch_shapes=[pltpu.VMEM((1, LANES), F32)],
        name="kvq")


ATT_SCORE_LEAD = 2


def _attn_scores(j, hd, slot, k_ref, kb_ref, qa_ref, s_ref, bm_ref):
    t = ATT_T
    off = pl.multiple_of(j * t, t)
    p = hd // 2
    kcat = jnp.concatenate([k_ref[0, pl.ds(off, t), p * LANES:(p + 1) * LANES],
                            kb_ref[0, pl.ds(off, t), :]], axis=1)
    s = _dot(kcat, qa_ref[hd])
    s_ref[slot, hd] = s
    bm_ref[slot, hd:hd + 1, :] = jnp.max(s, axis=0, keepdims=True)


def _attn_update(j, hd, slot, masked, vt_ref, s_ref, bm_ref, m_ref, acc_ref):
    t = ATT_T
    s = s_ref[slot, hd]
    m = m_ref[hd:hd + 1, :]
    if masked:
        causal = (lax.broadcasted_iota(jnp.int32, (t, t), 0)
                  <= lax.broadcasted_iota(jnp.int32, (t, t), 1))
        s = jnp.where(causal, s, NEG_BIG)
        m_new = jnp.maximum(m, jnp.max(s, axis=0, keepdims=True))
    else:
        m_new = jnp.maximum(m, bm_ref[slot, hd:hd + 1, :])
    pr = jnp.exp2(s - m_new)
    alpha = jnp.exp2(m - m_new)
    m_ref[hd:hd + 1, :] = m_new
    vth = vt_ref[0, j, hd * VT_ROWS:(hd + 1) * VT_ROWS, :]
    acc_ref[hd] = alpha * acc_ref[hd] + _dot(vth, pr.astype(BF))


def _attn_kernel(q_ref, k_ref, kb_ref, vt_ref, f2t_ref, sog_ref, y_ref,
                 qa_ref, s_ref, bm_ref, m_ref, acc_ref):
    t = ATT_T
    qi = pl.program_id(1)

    @pl.when(jnp.logical_and(pl.program_id(0) == 0, qi == 0))
    def _():
        row = lax.broadcasted_iota(jnp.int32, (2 * LANES, t), 0) - LANES
        for hd in range(FOX_HEADS):
            sel = (jnp.where(row == hd, 1.0, 0.0)
                   + jnp.where(row == FOX_HEADS + hd, 1.0, 0.0)
                   + jnp.where(row == 2 * FOX_HEADS + hd, 1.0, 0.0))
            qa_ref[hd] = sel.astype(BF)

    f_row0 = LANES + BIAS_ONE0
    f_pad = jnp.zeros((VT_PAD - 3, t), F32)
    for p in range(FOX_HEADS // 2):
        cs = slice(p * LANES, (p + 1) * LANES)
        qt = q_ref[0, :, cs].T
        for hh in range(2):
            hd = 2 * p + hh
            qa_ref[hd, hh * FOX_DH:(hh + 1) * FOX_DH, :] = qt[hh * FOX_DH:(hh + 1) * FOX_DH]
            fhi, fmid, flo = _split3(f2t_ref[0, hd:hd + 1, :])
            qa_ref[hd, f_row0:f_row0 + VT_PAD, :] = jnp.concatenate(
                [fhi, fmid, flo, f_pad], axis=0).astype(BF)
            _attn_scores(0, hd, 0, k_ref, kb_ref, qa_ref, s_ref, bm_ref)

    m_ref[...] = jnp.full(m_ref.shape, NEG_BIG, F32)
    acc_ref[...] = jnp.zeros_like(acc_ref)

    def stage(j, cur):
        for g in range(FOX_HEADS + ATT_SCORE_LEAD):
            if g < FOX_HEADS:
                _attn_scores(j + 1, g, 1 - cur, k_ref, kb_ref, qa_ref, s_ref, bm_ref)
            if g >= ATT_SCORE_LEAD:
                _attn_update(j, g - ATT_SCORE_LEAD, cur, False, vt_ref, s_ref, bm_ref, m_ref,
                             acc_ref)

    def last(cur):
        for p in range(FOX_HEADS // 2):
            halves = []
            for hd in (2 * p, 2 * p + 1):
                _attn_update(qi, hd, cur, True, vt_ref, s_ref, bm_ref, m_ref, acc_ref)
                num = acc_ref[hd, :FOX_DH, :]
                den = acc_ref[hd, FOX_DH:FOX_DH + 1, :]
                halves.append(num * (1.0 / den))
            cs = slice(p * LANES, (p + 1) * LANES)
            ot = jnp.concatenate(halves, axis=0)
            y_ref[0, :, cs] = (sog_ref[0, :, cs].astype(F32) * ot.T).astype(BF)

    def body(i, carry):
        stage(2 * i, 0)
        stage(2 * i + 1, 1)
        return carry

    lax.fori_loop(0, lax.shift_right_logical(qi, 1), body, 0)
    odd = lax.bitwise_and(qi, 1) == 1

    @pl.when(odd)
    def _():
        stage(qi - 1, 0)
        last(1)

    @pl.when(jnp.logical_not(odd))
    def _():
        last(0)


def _attn_call(q, k, kb, vt, f2t, sog):
    B, S, Dm = q.shape
    t = ATT_T
    tok = lambda n: pl.BlockSpec((1, t, n), lambda b, s: (b, s, 0))
    return pl.pallas_call(
        _attn_kernel,
        grid=(B, S // t),
        in_specs=[
            tok(Dm),
            pl.BlockSpec((1, S, Dm), lambda b, s: (b, 0, 0)),
            pl.BlockSpec((1, S, LANES), lambda b, s: (b, 0, 0)),
            pl.BlockSpec((1, S // t, FOX_HEADS * VT_ROWS, t), lambda b, s: (b, 0, 0, 0)),
            pl.BlockSpec((1, FOX_HEADS, t), lambda b, s: (b, 0, s)),
            tok(Dm),
        ],
        out_specs=tok(Dm),
        out_shape=jax.ShapeDtypeStruct((B, S, Dm), BF),
        scratch_shapes=[
            pltpu.VMEM((FOX_HEADS, 2 * LANES, t), BF),
            pltpu.VMEM((2, FOX_HEADS, t, t), F32),
            pltpu.VMEM((2, FOX_HEADS, t), F32),
            pltpu.VMEM((FOX_HEADS, t), F32),
            pltpu.VMEM((FOX_HEADS, VT_ROWS, t), F32),
        ],
        compiler_params=_params(("arbitrary", "arbitrary")),
        name="fox_attn",
    )(q, k, kb, vt, f2t, sog)


def _retention_constants():
    gam = 1.0 - np.power(2.0, -5.0 - np.arange(RET_HEADS, dtype=np.float64))
    log_g = np.log(gam)
    idx = np.arange(RET_CHUNK, dtype=np.float64)
    rel = idx[:, None] - idx[None, :]
    dec = np.where(rel >= 0, np.exp(log_g[:, None, None] * np.maximum(rel, 0.0)), 0.0)
    qd = np.exp(log_g[:, None] * (idx + 1.0))[:, :, None]
    kd = np.exp(log_g[:, None] * (RET_CHUNK - 1.0 - idx))[:, :, None]
    chunk_decay = tuple(float(v) for v in np.exp(log_g * RET_CHUNK))
    return (jnp.asarray(dec, F32), jnp.asarray(qd, F32), jnp.asarray(kd, F32), chunk_decay)


def kernel(x, c, positions, norm_mix_gain, norm_mlp_gain, w_ada, b_ada, w_mlp_in, w_mlp_out,
           ret_w_in, ret_norm_gain, ret_w_out, kv_norm_gain, kv_w_ada, kv_b_ada, kv_w,
           forget_bias, k_norm_gain, fox_w_in, q_norm_gain, fox_w_out):
    B, S, Dm = x.shape

    ada = _ada_call(c, w_ada, b_ada[:, None, :])
    ada = ada.reshape(2, B, 6, Dm)
    ada_kv = _ada_call(c, kv_w_ada[None], kv_b_ada[None, None, :])
    ada_kv = ada_kv.reshape(B, 2, Dm)

    rwi = ret_w_in.astype(BF)
    wf = kv_w[:, 2 * Dm:]
    wf3 = jnp.pad(jnp.tile(wf, (1, 3)), ((0, 0), (0, LANES - BIAS_ONE0))).astype(BF)

    half = RET_QK // 2
    invf = (ROPE_BASE ** (-jnp.arange(half, dtype=F32) / half))[None, :]
    dec, qd, kd, chunk_decay = _retention_constants()
    y, (w1, w2, rwo) = _ret_layer_call(
        x, ada[0], norm_mix_gain[0:1], positions.reshape(B, S // RET_CHUNK, 1, RET_CHUNK), invf, rwi,
        ret_norm_gain[0].reshape(1, RET_V_W), dec, qd, kd, chunk_decay,
        [(w_mlp_in, 0, None), (w_mlp_out, 0, None), (ret_w_out, 0, None)])
    x, (kvw, fwi) = _mlp_call(x, y, ada[0], norm_mlp_gain[0:1], rwo, w1, w2, "mlp0",
                              [(kv_w, None, 2 * Dm), (fox_w_in, 0, None)])

    fb3 = jnp.pad(jnp.tile(forget_bias, 3), (0, LANES - BIAS_ONE0))[None, :]
    kg = jnp.tile(k_norm_gain, FOX_HEADS)[None, :]
    qg = jnp.tile(q_norm_gain[0], FOX_HEADS)[None, :]
    (ks, kb, vt, f2t, qf, sog), (w1, w2, fwo) = _kvq_call(
        x, ada_kv, ada[1], kv_norm_gain[None, :], norm_mix_gain[1:2], kvw, fwi, wf3, fb3, kg, qg,
        [(w_mlp_in, 1, None), (w_mlp_out, 1, None), (fox_w_out, 0, None)])
    y = _attn_call(qf, ks, kb, vt, f2t, sog)
    x, _ = _mlp_call(x, y, ada[1], norm_mlp_gain[1:2], fwo, w1, w2, "mlp1", [])
    return x
```

```python
import functools
import math

import jax
import jax.numpy as jnp
import numpy as np
from jax import lax
from jax.experimental import pallas as pl
from jax.experimental.pallas import tpu as pltpu

F32 = jnp.float32
BF = jnp.bfloat16

D_MODEL = 1024
RET_HEADS = 4
RET_QK = 256
RET_V = 512
RET_CHUNK = 256
FOX_HEADS = 16
FOX_DH = 64
D_FF = 4096
EPS = 1e-6
ROPE_BASE = 10000.0
LOG2E = math.log2(math.e)
NEG_BIG = -1e30
LANES = 128
MXU_DIM = 256
VMEM_LIMIT = 48 * 1024 * 1024

BIAS_ONE0 = 3 * FOX_HEADS


def _params(sem):
    return pltpu.CompilerParams(dimension_semantics=sem, vmem_limit_bytes=VMEM_LIMIT)


def _resident(shape, index_map):
    return pl.BlockSpec(shape, index_map, pipeline_mode=pl.Buffered(1))


def _ln_mod(x, gain, shift, scale):
    ms = jnp.mean(x * x, axis=-1, keepdims=True)
    y = x * lax.rsqrt(ms + EPS)
    return (y * gain) * (1.0 + scale) + shift


def _dot(a, b):
    return jnp.dot(a, b, preferred_element_type=F32)


def _dot_nt(a, b):
    return lax.dot_general(a, b, (((1,), (1,)), ((), ())), preferred_element_type=F32)


def _split3(a):
    hi = a.astype(BF).astype(F32)
    r1 = a - hi
    mid = r1.astype(BF).astype(F32)
    lo = (r1 - mid).astype(BF).astype(F32)
    return hi, mid, lo


def _cast_plain(v):
    return (v.astype(BF),)


def _cast_kv_w(v):
    kv_cols = 2 * D_MODEL
    src = lax.broadcasted_iota(jnp.int32, (FOX_HEADS, LANES), 0)
    dst = lax.broadcasted_iota(jnp.int32, (FOX_HEADS, LANES), 1)
    same_head = lax.bitwise_and(dst, FOX_HEADS - 1) == src
    place = jnp.where(dst < BIAS_ONE0, jnp.where(same_head, 1.0, 0.0), 0.0).astype(BF)
    wf3 = _dot(v[:, kv_cols:].astype(BF), place).astype(BF)
    return v[:, :kv_cols].astype(BF), wf3


def _with_casts(body, n_in, n_out, fns, n_cast_out):
    n_cast = len(fns)

    def kernel(*refs):
        ins = refs[:n_in]
        cast_in = refs[n_in:n_in + n_cast]
        outs = refs[n_in + n_cast:n_in + n_cast + n_out]
        first_scratch = n_in + n_cast + n_out + n_cast_out
        cast_out = list(refs[n_in + n_cast + n_out:first_scratch])
        body(*ins, *outs, *refs[first_scratch:])
        for src, fn in zip(cast_in, fns):
            for val in fn(src[...]):
                cast_out.pop(0)[...] = val
    return kernel


def _cast_job(w, layer, nsteps, lin, fn, out_cols):
    R, C = w.shape[-2:]
    rows = R // nsteps
    if layer is None:
        in_spec = pl.BlockSpec((rows, C), lambda *g: (lin(*g), 0))
    else:
        in_spec = pl.BlockSpec((None, rows, C), lambda *g: (layer, lin(*g), 0))
    out_specs = [pl.BlockSpec((rows, c), lambda *g: (lin(*g), 0)) for c in out_cols]
    return in_spec, out_specs, [jax.ShapeDtypeStruct((R, c), BF) for c in out_cols], fn


def _call_with_casts(body, n_in, grid, in_specs, out_specs, out_shape, jobs, args, **kw):
    nsteps = grid[0] * grid[1]
    lin = lambda b, s: b * grid[1] + s
    built = []
    for job in jobs:
        w, layer = job[:2]
        fn, out_cols = job[2:] if len(job) == 4 else (_cast_plain, (w.shape[-1],))
        built.append(_cast_job(w, layer, nsteps, lin, fn, out_cols))
    out_specs = list(out_specs)
    out_shape = list(out_shape)
    cast_specs = [s for b in built for s in b[1]]
    cast_shapes = [s for b in built for s in b[2]]
    res = pl.pallas_call(
        _with_casts(body, n_in, len(out_specs), [b[3] for b in built], len(cast_specs)),
        grid=grid,
        in_specs=list(in_specs) + [b[0] for b in built],
        out_specs=out_specs + cast_specs,
        out_shape=out_shape + cast_shapes,
        compiler_params=_params(("arbitrary", "arbitrary")),
        **kw,
    )(*args, *[job[0] for job in jobs])
    n_out = len(out_specs)
    return list(res[:n_out]), list(res[n_out:])


def _ada_kernel(c_ref, w_ref, b_ref, o_ref):
    c = c_ref[...]
    ca = c * jax.nn.sigmoid(c)
    o_ref[...] = _dot(ca.astype(BF), w_ref[...].astype(BF)) + b_ref[...]


def _ada_call(c, w, b):
    L, Dm, N = w.shape
    B = c.shape[0]
    tn = 1024
    return pl.pallas_call(
        _ada_kernel,
        grid=(L, N // tn),
        in_specs=[
            pl.BlockSpec((B, Dm), lambda l, n: (0, 0)),
            pl.BlockSpec((None, Dm, tn), lambda l, n: (l, 0, n)),
            pl.BlockSpec((None, 1, tn), lambda l, n: (l, 0, n)),
        ],
        out_specs=pl.BlockSpec((None, B, tn), lambda l, n: (l, 0, n)),
        out_shape=jax.ShapeDtypeStruct((L, B, N), F32),
        compiler_params=_params(("arbitrary", "arbitrary")),
        name="ada",
    )(c, w, b)


RET_BATCH = 2
RET_STEP_CHUNKS = 1
RET_QK_W = RET_HEADS * RET_QK
RET_V_W = RET_HEADS * RET_V


def _ret_project_pieces(bb, c, x_ref, ada_ref, gain_ref, pos_ref, invf_ref, w_ref,
                        q_s, k_s, v_s, sg_s):
    rows = slice(c * RET_CHUNK, (c + 1) * RET_CHUNK)
    half = RET_QK // 2
    ctx = {}

    def rotary(dst, base, mul):
        qk = _dot(ctx["h"], w_ref[:, base: base + RET_QK_W])
        cos, sin = ctx["cos"], ctx["sin"]
        for hh in range(RET_HEADS):
            lo = hh * RET_QK
            x1 = qk[:, lo: lo + half]
            x2 = qk[:, lo + half: lo + RET_QK]
            dst[bb, :, lo: lo + half] = ((x1 * cos - x2 * sin) * mul).astype(BF)
            dst[bb, :, lo + half: lo + RET_QK] = ((x1 * sin + x2 * cos) * mul).astype(BF)

    def p_v():
        ada = ada_ref[bb]
        x = x_ref[bb, rows, :]
        xn = x * lax.rsqrt(jnp.mean(x * x, axis=-1, keepdims=True) + EPS)
        ctx["h"] = (xn * (gain_ref[...] * (1.0 + ada[1:2])) + ada[0:1]).astype(BF)
        pos_col = jnp.broadcast_to(pos_ref[bb, c].astype(F32), (RET_QK // 2, RET_CHUNK)).T
        ang = pos_col * invf_ref[...]
        ctx["cos"] = jnp.cos(ang)
        ctx["sin"] = jnp.sin(ang)
        v_s[bb] = _dot(ctx["h"], w_ref[:, 2 * RET_QK_W: 2 * RET_QK_W + RET_V_W]).astype(BF)

    def p_g():
        g = _dot(ctx["h"], w_ref[:, 2 * RET_QK_W + RET_V_W:])
        sg_s[bb] = (g * jax.nn.sigmoid(g)).astype(BF)

    def p_q():
        rotary(q_s, 0, 1.0)

    def p_k():
        rotary(k_s, RET_QK_W, RET_QK ** -0.5)

    return [p_v, p_g, p_q, p_k]


def _ret_head(bb, c, hh, chunk_decay, q_s, k_s, v_s, sg_s, gn_ref, dec_ref, qd_ref, kd_ref,
              y_ref, state_ref):
    rows = slice(c * RET_CHUNK, (c + 1) * RET_CHUNK)
    qs = slice(hh * RET_QK, (hh + 1) * RET_QK)
    vs = slice(hh * RET_V, (hh + 1) * RET_V)
    q = q_s[bb, :, qs]
    k = k_s[bb, :, qs]
    v = v_s[bb, :, vs]
    s = _dot_nt(q, k) * dec_ref[hh]
    intra = _dot(s.astype(BF), v)
    st = state_ref[bb, hh]
    cross = _dot(q, st.astype(BF)) * qd_ref[hh]
    kdt = (k.astype(F32) * kd_ref[hh]).T.astype(BF)
    state_ref[bb, hh] = st * chunk_decay[hh] + _dot(kdt, v)
    y = intra + cross
    ms = jnp.mean(y * y, axis=-1, keepdims=True)
    yn = ((y * lax.rsqrt(ms + EPS)) * gn_ref[:, vs]).astype(BF)
    y_ref[bb, rows, vs] = sg_s[bb, :, vs] * yn


def _ret_layer_kernel(chunk_decay, x_ref, ada_ref, gain_ref, pos_ref, invf_ref, w_ref,
                      gn_ref, dec_ref, qd_ref, kd_ref, y_ref,
                      state_ref, q_s, k_s, v_s, sg_s):
    @pl.when(pl.program_id(1) == 0)
    def _():
        state_ref[...] = jnp.zeros_like(state_ref)

    proj = (x_ref, ada_ref, gain_ref, pos_ref, invf_ref, w_ref, q_s, k_s, v_s, sg_s)
    head = (chunk_decay, q_s, k_s, v_s, sg_s, gn_ref, dec_ref, qd_ref, kd_ref, y_ref, state_ref)
    tiles = [(bb, c) for c in range(RET_STEP_CHUNKS) for bb in range(RET_BATCH)]
    for piece in _ret_project_pieces(*tiles[0], *proj):
        piece()
    for i, (bb, c) in enumerate(tiles):
        nxt = (_ret_project_pieces(*tiles[i + 1], *proj) if i + 1 < len(tiles)
               else [None] * RET_HEADS)
        for hh in range(RET_HEADS):
            if nxt[hh] is not None:
                nxt[hh]()
            _ret_head(bb, c, hh, *head)


def _ret_layer_call(x, ada, gain, pos, invf, w, gn, dec, qd, kd, chunk_decay, cast_jobs):
    B, S, Dm = x.shape
    ts = RET_STEP_CHUNKS * RET_CHUNK
    tok = lambda n: pl.BlockSpec((RET_BATCH, ts, n), lambda b, s: (b, s, 0))
    const = lambda shape: pl.BlockSpec(shape, lambda b, s: (0,) * len(shape))
    seq_scratch = lambda n: pltpu.VMEM((RET_BATCH, RET_CHUNK, n), BF)
    in_specs = [
        tok(Dm),
        pl.BlockSpec((RET_BATCH, 6, Dm), lambda b, s: (b, 0, 0)),
        const((1, Dm)),
        pl.BlockSpec((RET_BATCH, RET_STEP_CHUNKS, 1, RET_CHUNK), lambda b, s: (b, s, 0, 0)),
        const((1, RET_QK // 2)),
        _resident((None, Dm, 2 * RET_QK_W + 2 * RET_V_W), lambda b, s: (0, 0, 0)),
        const((1, RET_V_W)),
        const((RET_HEADS, RET_CHUNK, RET_CHUNK)),
        const((RET_HEADS, RET_CHUNK, 1)),
        const((RET_HEADS, RET_CHUNK, 1)),
    ]
    (y,), cast = _call_with_casts(
        functools.partial(_ret_layer_kernel, chunk_decay), len(in_specs),
        (B // RET_BATCH, S // ts), in_specs,
        [tok(RET_V_W)], [jax.ShapeDtypeStruct((B, S, RET_V_W), BF)], cast_jobs,
        (x, ada, gain, pos, invf, w, gn, dec, qd, kd),
        scratch_shapes=[
            pltpu.VMEM((RET_BATCH, RET_HEADS, RET_QK, RET_V), F32),
            seq_scratch(RET_QK_W), seq_scratch(RET_QK_W),
            seq_scratch(RET_V_W), seq_scratch(RET_V_W),
        ],
        name="ret_layer")
    return y, cast


def _mlp_kernel(x_ref, y_ref, ada_ref, gain_ref, wo_ref, w1_ref, w2_ref, o_ref, hid_ref):
    ada = ada_ref[0]
    mod = gain_ref[...] * (1.0 + ada[4:5])
    fc = 1024
    for sub in range(MLP_SUB):
        rows = slice(sub * MLP_TM, (sub + 1) * MLP_TM)
        x1 = x_ref[0, rows, :] + ada[2:3] * _dot(y_ref[0, rows, :], wo_ref[...])
        xn = x1 * lax.rsqrt(jnp.mean(x1 * x1, axis=-1, keepdims=True) + EPS)
        h = (xn * mod + ada[3:4]).astype(BF)
        for c in range(D_FF // fc):
            a = jnp.maximum(_dot(h, w1_ref[:, c * fc:(c + 1) * fc]), 0.0)
            hid_ref[rows, c * fc:(c + 1) * fc] = (a * a).astype(BF)
        o_ref[0, rows, :] = x1 + ada[5:6] * _dot(hid_ref[rows, :], w2_ref[...])


MLP_TM = 256
MLP_SUB = 2


def _mlp_call(x, y, ada, gain, wo, w1, w2, name, cast_jobs):
    B, S, Dm = x.shape
    Ky = y.shape[-1]
    tm = MLP_TM * MLP_SUB
    tok = lambda n: pl.BlockSpec((1, tm, n), lambda b, s: (b, s, 0))
    in_specs = [
        tok(Dm), tok(Ky),
        pl.BlockSpec((1, 6, Dm), lambda b, s: (b, 0, 0)),
        pl.BlockSpec((1, Dm), lambda b, s: (0, 0)),
        _resident((Ky, Dm), lambda b, s: (0, 0)),
        _resident((Dm, D_FF), lambda b, s: (0, 0)),
        _resident((D_FF, Dm), lambda b, s: (0, 0)),
    ]
    (out,), cast = _call_with_casts(
        _mlp_kernel, len(in_specs), (B, S // tm), in_specs,
        [tok(Dm)], [jax.ShapeDtypeStruct((B, S, Dm), F32)], cast_jobs,
        (x, y, ada, gain, wo, w1, w2),
        scratch_shapes=[pltpu.VMEM((tm, D_FF), BF)],
        name=name)
    return out, cast


def _group_ones(n, group):
    r = lax.broadcasted_iota(jnp.int32, (n, n), 0) // group
    c = lax.broadcasted_iota(jnp.int32, (n, n), 1) // group
    return jnp.where(r == c, 1.0, 0.0).astype(BF)


def _head_rms(a, ones_blk):
    ss = _dot((a * a).astype(BF), ones_blk)
    return lax.rsqrt(ss * (1.0 / FOX_DH) + EPS)


def _log_sigmoid(z):
    return jnp.minimum(z, 0.0) - jnp.log(1.0 + jnp.exp(-jnp.abs(z)))


ATT_T = 256
VT_PAD = 16
VT_ROWS = FOX_DH + VT_PAD


KVQ_SUB = 2
KVQ_STEP = KVQ_SUB * ATT_T


def _kvq_kernel(x_ref, adakv_ref, ada_ref, gkv_ref, gq_ref, wkv_ref, wq_ref, wf3_ref, fb3_ref,
                kg_ref, qg_ref,
                k_ref, kb_ref, vt_ref, f2t_ref, q_ref, sog_ref, carry_row):
    @pl.when(pl.program_id(1) == 0)
    def _():
        carry_row[...] = jnp.zeros_like(carry_row)

    tm = ATT_T
    adakv = adakv_ref[0]
    ada = ada_ref[0]
    mod_kv = gkv_ref[...] * (1.0 + adakv[1:2])
    mod_q = gq_ref[...] * (1.0 + ada[1:2])
    ones_blk = _group_ones(MXU_DIM, FOX_DH)
    qscale = (FOX_DH ** -0.5) * LOG2E
    ri = lax.broadcasted_iota(jnp.int32, (tm, tm), 0)
    ci = lax.broadcasted_iota(jnp.int32, (tm, tm), 1)
    lower = jnp.where(ci <= ri, 1.0, 0.0).astype(BF)
    lane = lax.broadcasted_iota(jnp.int32, (tm, LANES), 1)
    ones_lane = jnp.where(lane < BIAS_ONE0 + 3, 1.0, 0.0)
    pad_rows = jnp.where(lax.broadcasted_iota(jnp.int32, (VT_PAD, tm), 0) == 0, 1.0, 0.0).astype(BF)

    for sub in range(KVQ_SUB):
        rows = slice(sub * tm, (sub + 1) * tm)
        x = x_ref[0, rows, :]
        xn = x * lax.rsqrt(jnp.mean(x * x, axis=-1, keepdims=True) + EPS)
        h = (xn * mod_kv + adakv[0:1]).astype(BF)
        hq = (xn * mod_q + ada[0:1]).astype(BF)

        kv = _dot(h, wkv_ref[...])
        for sl in range(D_MODEL // MXU_DIM):
            cs = slice(sl * MXU_DIM, (sl + 1) * MXU_DIM)
            a = kv[:, cs]
            k_ref[0, rows, cs] = (a * _head_rms(a, ones_blk) * kg_ref[:, cs]).astype(BF)
        for c in range(2):
            vt = kv[:, D_MODEL + c * 512: D_MODEL + (c + 1) * 512].T.astype(BF)
            for i in range(512 // FOX_DH):
                r0 = (c * (512 // FOX_DH) + i) * VT_ROWS
                vt_ref[0, sub, r0:r0 + FOX_DH, :] = vt[i * FOX_DH:(i + 1) * FOX_DH]
                vt_ref[0, sub, r0 + FOX_DH:r0 + VT_ROWS, :] = pad_rows

        hi, mid, lo = _split3(_log_sigmoid(_dot(h, wf3_ref[...]) + fb3_ref[...]))
        cum = ((_dot(lower, hi.astype(BF)) + _dot(lower, mid.astype(BF)))
               + _dot(lower, lo.astype(BF)) + carry_row[...])
        carry_row[...] = cum[tm - 1:tm, :]
        f2 = cum * LOG2E
        hi, mid, lo = _split3(f2)
        kb = jnp.where(lane < FOX_HEADS, -hi,
                       jnp.where(lane < 2 * FOX_HEADS, -mid,
                                 jnp.where(lane < BIAS_ONE0, -lo, ones_lane)))
        kb_ref[0, rows, :] = kb.astype(BF)
        f2t_ref[0, :, rows] = f2.T[:FOX_HEADS, :]

        qo = _dot(hq, wq_ref[...])
        for sl in range(D_MODEL // MXU_DIM):
            cs = slice(sl * MXU_DIM, (sl + 1) * MXU_DIM)
            a = qo[:, cs]
            q_ref[0, rows, cs] = (a * _head_rms(a, ones_blk) * (qg_ref[:, cs] * qscale)).astype(BF)
        sog_ref[0, rows, :] = jax.nn.sigmoid(qo[:, D_MODEL:]).astype(BF)


def _kvq_call(x, ada_kv, ada, gain_kv, gain_q, wkv, wq, wf3, fb3, kg, qg, cast_jobs):
    B, S, Dm = x.shape
    ts = KVQ_STEP
    tok = lambda n: pl.BlockSpec((1, ts, n), lambda b, s: (b, s, 0))
    const = lambda shape: pl.BlockSpec(shape, lambda b, s: (0,) * len(shape))
    in_specs = [
        tok(Dm),
        pl.BlockSpec((1, 2, Dm), lambda b, s: (b, 0, 0)),
        pl.BlockSpec((1, 6, Dm), lambda b, s: (b, 0, 0)),
        const((1, Dm)), const((1, Dm)),
        _resident((Dm, 2 * Dm), lambda b, s: (0, 0)),
        _resident((Dm, 2 * Dm), lambda b, s: (0, 0)),
        const((Dm, LANES)),
        const((1, LANES)),
        const((1, Dm)), const((1, Dm)),
    ]
    out_specs = [
        tok(Dm), tok(LANES),
        pl.BlockSpec((1, KVQ_SUB, FOX_HEADS * VT_ROWS, ATT_T), lambda b, s: (b, s, 0, 0)),
        pl.BlockSpec((1, FOX_HEADS, ts), lambda b, s: (b, 0, s)),
        tok(Dm), tok(Dm),
    ]
    out_shape = [
        jax.ShapeDtypeStruct((B, S, Dm), BF),
        jax.ShapeDtypeStruct((B, S, LANES), BF),
        jax.ShapeDtypeStruct((B, S // ATT_T, FOX_HEADS * VT_ROWS, ATT_T), BF),
        jax.ShapeDtypeStruct((B, FOX_HEADS, S), F32),
        jax.ShapeDtypeStruct((B, S, Dm), BF),
        jax.ShapeDtypeStruct((B, S, Dm), BF),
    ]
    return _call_with_casts(
        _kvq_kernel, len(in_specs), (B, S // ts), in_specs, out_specs, out_shape, cast_jobs,
        (x, ada_kv, ada, gain_kv, gain_q, wkv, wq, wf3, fb3, kg, qg),
        scratch_shapes=[pltpu.VMEM((1, LANES), F32)],
        name="kvq")


ATT_SCORE_LEAD = 2


def _attn_scores(j, hd, slot, k_ref, kb_ref, qa_ref, s_ref, bm_ref):
    t = ATT_T
    off = pl.multiple_of(j * t, t)
    p = hd // 2
    kcat = jnp.concatenate([k_ref[0, pl.ds(off, t), p * LANES:(p + 1) * LANES],
                            kb_ref[0, pl.ds(off, t), :]], axis=1)
    s = _dot(kcat, qa_ref[hd])
    s_ref[slot, hd] = s
    bm_ref[slot, hd:hd + 1, :] = jnp.max(s, axis=0, keepdims=True)


def _attn_update(j, hd, slot, masked, vt_ref, s_ref, bm_ref, m_ref, acc_ref):
    t = ATT_T
    s = s_ref[slot, hd]
    m = m_ref[hd:hd + 1, :]
    if masked:
        causal = (lax.broadcasted_iota(jnp.int32, (t, t), 0)
                  <= lax.broadcasted_iota(jnp.int32, (t, t), 1))
        s = jnp.where(causal, s, NEG_BIG)
        m_new = jnp.maximum(m, jnp.max(s, axis=0, keepdims=True))
    else:
        m_new = jnp.maximum(m, bm_ref[slot, hd:hd + 1, :])
    pr = jnp.exp2(s - m_new)
    alpha = jnp.exp2(m - m_new)
    m_ref[hd:hd + 1, :] = m_new
    vth = vt_ref[0, j, hd * VT_ROWS:(hd + 1) * VT_ROWS, :]
    acc_ref[hd] = alpha * acc_ref[hd] + _dot(vth, pr.astype(BF))


def _attn_kernel(q_ref, k_ref, kb_ref, vt_ref, f2t_ref, sog_ref, y_ref,
                 qa_ref, s_ref, bm_ref, m_ref, acc_ref):
    t = ATT_T
    qi = pl.program_id(1)

    @pl.when(jnp.logical_and(pl.program_id(0) == 0, qi == 0))
    def _():
        row = lax.broadcasted_iota(jnp.int32, (2 * LANES, t), 0) - LANES
        for hd in range(FOX_HEADS):
            sel = (jnp.where(row == hd, 1.0, 0.0)
                   + jnp.where(row == FOX_HEADS + hd, 1.0, 0.0)
                   + jnp.where(row == 2 * FOX_HEADS + hd, 1.0, 0.0))
            qa_ref[hd] = sel.astype(BF)

    f_row0 = LANES + BIAS_ONE0
    f_pad = jnp.zeros((VT_PAD - 3, t), F32)
    for p in range(FOX_HEADS // 2):
        cs = slice(p * LANES, (p + 1) * LANES)
        qt = q_ref[0, :, cs].T
        for hh in range(2):
            hd = 2 * p + hh
            qa_ref[hd, hh * FOX_DH:(hh + 1) * FOX_DH, :] = qt[hh * FOX_DH:(hh + 1) * FOX_DH]
            fhi, fmid, flo = _split3(f2t_ref[0, hd:hd + 1, :])
            qa_ref[hd, f_row0:f_row0 + VT_PAD, :] = jnp.concatenate(
                [fhi, fmid, flo, f_pad], axis=0).astype(BF)
            _attn_scores(0, hd, 0, k_ref, kb_ref, qa_ref, s_ref, bm_ref)

    m_ref[...] = jnp.full(m_ref.shape, NEG_BIG, F32)
    acc_ref[...] = jnp.zeros_like(acc_ref)

    def stage(j, cur):
        for g in range(FOX_HEADS + ATT_SCORE_LEAD):
            if g < FOX_HEADS:
                _attn_scores(j + 1, g, 1 - cur, k_ref, kb_ref, qa_ref, s_ref, bm_ref)
            if g >= ATT_SCORE_LEAD:
                _attn_update(j, g - ATT_SCORE_LEAD, cur, False, vt_ref, s_ref, bm_ref, m_ref,
                             acc_ref)

    def last(cur):
        for p in range(FOX_HEADS // 2):
            halves = []
            for hd in (2 * p, 2 * p + 1):
                _attn_update(qi, hd, cur, True, vt_ref, s_ref, bm_ref, m_ref, acc_ref)
                num = acc_ref[hd, :FOX_DH, :]
                den = acc_ref[hd, FOX_DH:FOX_DH + 1, :]
                halves.append(num * (1.0 / den))
            cs = slice(p * LANES, (p + 1) * LANES)
            ot = jnp.concatenate(halves, axis=0)
            y_ref[0, :, cs] = (sog_ref[0, :, cs].astype(F32) * ot.T).astype(BF)

    def body(i, carry):
        stage(2 * i, 0)
        stage(2 * i + 1, 1)
        return carry

    lax.fori_loop(0, lax.shift_right_logical(qi, 1), body, 0)
    odd = lax.bitwise_and(qi, 1) == 1

    @pl.when(odd)
    def _():
        stage(qi - 1, 0)
        last(1)

    @pl.when(jnp.logical_not(odd))
    def _():
        last(0)


def _attn_call(q, k, kb, vt, f2t, sog):
    B, S, Dm = q.shape
    t = ATT_T
    tok = lambda n: pl.BlockSpec((1, t, n), lambda b, s: (b, s, 0))
    return pl.pallas_call(
        _attn_kernel,
        grid=(B, S // t),
        in_specs=[
            tok(Dm),
            pl.BlockSpec((1, S, Dm), lambda b, s: (b, 0, 0)),
            pl.BlockSpec((1, S, LANES), lambda b, s: (b, 0, 0)),
            pl.BlockSpec((1, S // t, FOX_HEADS * VT_ROWS, t), lambda b, s: (b, 0, 0, 0)),
            pl.BlockSpec((1, FOX_HEADS, t), lambda b, s: (b, 0, s)),
            tok(Dm),
        ],
        out_specs=tok(Dm),
        out_shape=jax.ShapeDtypeStruct((B, S, Dm), BF),
        scratch_shapes=[
            pltpu.VMEM((FOX_HEADS, 2 * LANES, t), BF),
            pltpu.VMEM((2, FOX_HEADS, t, t), F32),
            pltpu.VMEM((2, FOX_HEADS, t), F32),
            pltpu.VMEM((FOX_HEADS, t), F32),
            pltpu.VMEM((FOX_HEADS, VT_ROWS, t), F32),
        ],
        compiler_params=_params(("arbitrary", "arbitrary")),
        name="fox_attn",
    )(q, k, kb, vt, f2t, sog)


def _retention_constants():
    gam = 1.0 - np.power(2.0, -5.0 - np.arange(RET_HEADS, dtype=np.float64))
    log_g = np.log(gam)
    idx = np.arange(RET_CHUNK, dtype=np.float64)
    rel = idx[:, None] - idx[None, :]
    dec = np.where(rel >= 0, np.exp(log_g[:, None, None] * np.maximum(rel, 0.0)), 0.0)
    qd = np.exp(log_g[:, None] * (idx + 1.0))[:, :, None]
    kd = np.exp(log_g[:, None] * (RET_CHUNK - 1.0 - idx))[:, :, None]
    chunk_decay = tuple(float(v) for v in np.exp(log_g * RET_CHUNK))
    return (jnp.asarray(dec, F32), jnp.asarray(qd, F32), jnp.asarray(kd, F32), chunk_decay)


def kernel(x, c, positions, norm_mix_gain, norm_mlp_gain, w_ada, b_ada, w_mlp_in, w_mlp_out,
           ret_w_in, ret_norm_gain, ret_w_out, kv_norm_gain, kv_w_ada, kv_b_ada, kv_w,
           forget_bias, k_norm_gain, fox_w_in, q_norm_gain, fox_w_out):
    B, S, Dm = x.shape

    ada = _ada_call(c, w_ada, b_ada[:, None, :])
    ada = ada.reshape(2, B, 6, Dm)
    ada_kv = _ada_call(c, kv_w_ada[None], kv_b_ada[None, None, :])
    ada_kv = ada_kv.reshape(B, 2, Dm)

    rwi = ret_w_in.astype(BF)

    half = RET_QK // 2
    invf = (ROPE_BASE ** (-jnp.arange(half, dtype=F32) / half))[None, :]
    dec, qd, kd, chunk_decay = _retention_constants()
    y, (w1, w2, rwo) = _ret_layer_call(
        x, ada[0], norm_mix_gain[0:1], positions.reshape(B, S // RET_CHUNK, 1, RET_CHUNK), invf, rwi,
        ret_norm_gain[0].reshape(1, RET_V_W), dec, qd, kd, chunk_decay,
        [(w_mlp_in, 0), (w_mlp_out, 0), (ret_w_out, 0)])
    x, (kvw, wf3, fwi) = _mlp_call(
        x, y, ada[0], norm_mlp_gain[0:1], rwo, w1, w2, "mlp0",
        [(kv_w, None, _cast_kv_w, (2 * Dm, LANES)), (fox_w_in, 0)])

    fb3 = jnp.pad(jnp.tile(forget_bias, 3), (0, LANES - BIAS_ONE0))[None, :]
    kg = jnp.tile(k_norm_gain, FOX_HEADS)[None, :]
    qg = jnp.tile(q_norm_gain[0], FOX_HEADS)[None, :]
    (ks, kb, vt, f2t, qf, sog), (w1, w2, fwo) = _kvq_call(
        x, ada_kv, ada[1], kv_norm_gain[None, :], norm_mix_gain[1:2], kvw, fwi, wf3, fb3, kg, qg,
        [(w_mlp_in, 1), (w_mlp_out, 1), (fox_w_out, 0)])
    y = _attn_call(qf, ks, kb, vt, f2t, sog)
    x, _ = _mlp_call(x, y, ada[1], norm_mlp_gain[1:2], fwo, w1, w2, "mlp1", [])
    return x
```

```python
import functools
import math

import jax
import jax.numpy as jnp
import numpy as np
from jax import lax
from jax.experimental import pallas as pl
from jax.experimental.pallas import tpu as pltpu

F32 = jnp.float32
BF = jnp.bfloat16

D_MODEL = 1024
RET_HEADS = 4
RET_QK = 256
RET_V = 512
RET_CHUNK = 256
FOX_HEADS = 16
FOX_DH = 64
D_FF = 4096
EPS = 1e-6
ROPE_BASE = 10000.0
LOG2E = math.log2(math.e)
NEG_BIG = -1e30
LANES = 128
MXU_DIM = 256
VMEM_LIMIT = 48 * 1024 * 1024

BIAS_ONE0 = 3 * FOX_HEADS


def _params(sem):
    return pltpu.CompilerParams(dimension_semantics=sem, vmem_limit_bytes=VMEM_LIMIT)


def _resident(shape, index_map):
    return pl.BlockSpec(shape, index_map, pipeline_mode=pl.Buffered(1))


def _ln_mod(x, gain, shift, scale):
    ms = jnp.mean(x * x, axis=-1, keepdims=True)
    y = x * lax.rsqrt(ms + EPS)
    return (y * gain) * (1.0 + scale) + shift


def _dot(a, b):
    return jnp.dot(a, b, preferred_element_type=F32)


def _dot_nt(a, b):
    return lax.dot_general(a, b, (((1,), (1,)), ((), ())), preferred_element_type=F32)


def _split3(a):
    hi = a.astype(BF).astype(F32)
    r1 = a - hi
    mid = r1.astype(BF).astype(F32)
    lo = (r1 - mid).astype(BF).astype(F32)
    return hi, mid, lo


def _rows_job(w, layer=None):
    def build(nsteps, lin):
        R, C = w.shape[-2:]
        rows = R // nsteps
        if layer is None:
            in_spec = pl.BlockSpec((rows, C), lambda *g: (lin(*g), 0))
        else:
            in_spec = pl.BlockSpec((None, rows, C), lambda *g: (layer, lin(*g), 0))
        out_spec = pl.BlockSpec((rows, C), lambda *g: (lin(*g), 0))
        return (w, in_spec, [out_spec], [jax.ShapeDtypeStruct((R, C), BF)],
                lambda v: (v.astype(BF),))
    return build


def _kv_proj_job(kv_wt):
    def build(nsteps, lin):
        Dm = kv_wt.shape[1]
        last = 2 * Dm // LANES - 1
        blk = lambda *g: jnp.minimum(lin(*g), last)
        return (kv_wt,
                pl.BlockSpec((LANES, Dm), lambda *g: (blk(*g), 0)),
                [pl.BlockSpec((Dm, LANES), lambda *g: (0, blk(*g)))],
                [jax.ShapeDtypeStruct((Dm, 2 * Dm), BF)],
                lambda v: (v.T.astype(BF),))
    return build


def _with_casts(body, n_in, n_out, fns, n_cast_out):
    n_cast = len(fns)

    def kernel(*refs):
        ins = refs[:n_in]
        cast_in = refs[n_in:n_in + n_cast]
        outs = refs[n_in + n_cast:n_in + n_cast + n_out]
        first_scratch = n_in + n_cast + n_out + n_cast_out
        cast_out = list(refs[n_in + n_cast + n_out:first_scratch])
        body(*ins, *outs, *refs[first_scratch:])
        for src, fn in zip(cast_in, fns):
            for val in fn(src[...]):
                cast_out.pop(0)[...] = val
    return kernel


def _call_with_casts(body, n_in, grid, in_specs, out_specs, out_shape, jobs, args, **kw):
    nsteps = grid[0] * grid[1]
    lin = lambda b, s: b * grid[1] + s
    built = [job(nsteps, lin) for job in jobs]
    out_specs = list(out_specs)
    out_shape = list(out_shape)
    cast_specs = [s for b in built for s in b[2]]
    cast_shapes = [s for b in built for s in b[3]]
    res = pl.pallas_call(
        _with_casts(body, n_in, len(out_specs), [b[4] for b in built], len(cast_specs)),
        grid=grid,
        in_specs=list(in_specs) + [b[1] for b in built],
        out_specs=out_specs + cast_specs,
        out_shape=out_shape + cast_shapes,
        compiler_params=_params(("arbitrary", "arbitrary")),
        **kw,
    )(*args, *[b[0] for b in built])
    n_out = len(out_specs)
    return list(res[:n_out]), list(res[n_out:])


def _ada_kernel(c_ref, w_ref, b_ref, o_ref):
    c = c_ref[...]
    ca = c * jax.nn.sigmoid(c)
    o_ref[...] = _dot(ca.astype(BF), w_ref[...].astype(BF)) + b_ref[...]


def _ada_call(c, w, b):
    L, Dm, N = w.shape
    B = c.shape[0]
    tn = 1024
    return pl.pallas_call(
        _ada_kernel,
        grid=(L, N // tn),
        in_specs=[
            pl.BlockSpec((B, Dm), lambda l, n: (0, 0)),
            pl.BlockSpec((None, Dm, tn), lambda l, n: (l, 0, n)),
            pl.BlockSpec((None, 1, tn), lambda l, n: (l, 0, n)),
        ],
        out_specs=pl.BlockSpec((None, B, tn), lambda l, n: (l, 0, n)),
        out_shape=jax.ShapeDtypeStruct((L, B, N), F32),
        compiler_params=_params(("arbitrary", "arbitrary")),
        name="ada",
    )(c, w, b)


RET_BATCH = 2
RET_STEP_CHUNKS = 1
RET_QK_W = RET_HEADS * RET_QK
RET_V_W = RET_HEADS * RET_V


def _ret_project_pieces(bb, c, x_ref, ada_ref, gain_ref, pos_ref, invf_ref, w_ref,
                        q_s, k_s, v_s, sg_s):
    rows = slice(c * RET_CHUNK, (c + 1) * RET_CHUNK)
    half = RET_QK // 2
    ctx = {}

    def rotary(dst, base, mul):
        qk = _dot(ctx["h"], w_ref[:, base: base + RET_QK_W])
        cos, sin = ctx["cos"], ctx["sin"]
        for hh in range(RET_HEADS):
            lo = hh * RET_QK
            x1 = qk[:, lo: lo + half]
            x2 = qk[:, lo + half: lo + RET_QK]
            dst[bb, :, lo: lo + half] = ((x1 * cos - x2 * sin) * mul).astype(BF)
            dst[bb, :, lo + half: lo + RET_QK] = ((x1 * sin + x2 * cos) * mul).astype(BF)

    def p_v():
        ada = ada_ref[bb]
        x = x_ref[bb, rows, :]
        xn = x * lax.rsqrt(jnp.mean(x * x, axis=-1, keepdims=True) + EPS)
        ctx["h"] = (xn * (gain_ref[...] * (1.0 + ada[1:2])) + ada[0:1]).astype(BF)
        pos_col = jnp.broadcast_to(pos_ref[bb, c].astype(F32), (RET_QK // 2, RET_CHUNK)).T
        ang = pos_col * invf_ref[...]
        ctx["cos"] = jnp.cos(ang)
        ctx["sin"] = jnp.sin(ang)
        v_s[bb] = _dot(ctx["h"], w_ref[:, 2 * RET_QK_W: 2 * RET_QK_W + RET_V_W]).astype(BF)

    def p_g():
        g = _dot(ctx["h"], w_ref[:, 2 * RET_QK_W + RET_V_W:])
        sg_s[bb] = (g * jax.nn.sigmoid(g)).astype(BF)

    def p_q():
        rotary(q_s, 0, 1.0)

    def p_k():
        rotary(k_s, RET_QK_W, RET_QK ** -0.5)

    return [p_v, p_g, p_q, p_k]


def _ret_head(bb, c, hh, chunk_decay, q_s, k_s, v_s, sg_s, gn_ref, dec_ref, qd_ref, kd_ref,
              y_ref, state_ref):
    rows = slice(c * RET_CHUNK, (c + 1) * RET_CHUNK)
    qs = slice(hh * RET_QK, (hh + 1) * RET_QK)
    vs = slice(hh * RET_V, (hh + 1) * RET_V)
    q = q_s[bb, :, qs]
    k = k_s[bb, :, qs]
    v = v_s[bb, :, vs]
    s = _dot_nt(q, k) * dec_ref[hh]
    intra = _dot(s.astype(BF), v)
    st = state_ref[bb, hh]
    cross = _dot(q, st.astype(BF)) * qd_ref[hh]
    kdt = (k.astype(F32) * kd_ref[hh]).T.astype(BF)
    state_ref[bb, hh] = st * chunk_decay[hh] + _dot(kdt, v)
    y = intra + cross
    ms = jnp.mean(y * y, axis=-1, keepdims=True)
    yn = ((y * lax.rsqrt(ms + EPS)) * gn_ref[:, vs]).astype(BF)
    y_ref[bb, rows, vs] = sg_s[bb, :, vs] * yn


def _ret_layer_kernel(chunk_decay, x_ref, ada_ref, gain_ref, pos_ref, invf_ref, w_ref,
                      gn_ref, dec_ref, qd_ref, kd_ref, y_ref,
                      state_ref, q_s, k_s, v_s, sg_s):
    @pl.when(pl.program_id(1) == 0)
    def _():
        state_ref[...] = jnp.zeros_like(state_ref)

    proj = (x_ref, ada_ref, gain_ref, pos_ref, invf_ref, w_ref, q_s, k_s, v_s, sg_s)
    head = (chunk_decay, q_s, k_s, v_s, sg_s, gn_ref, dec_ref, qd_ref, kd_ref, y_ref, state_ref)
    tiles = [(bb, c) for c in range(RET_STEP_CHUNKS) for bb in range(RET_BATCH)]
    for piece in _ret_project_pieces(*tiles[0], *proj):
        piece()
    for i, (bb, c) in enumerate(tiles):
        nxt = (_ret_project_pieces(*tiles[i + 1], *proj) if i + 1 < len(tiles)
               else [None] * RET_HEADS)
        for hh in range(RET_HEADS):
            if nxt[hh] is not None:
                nxt[hh]()
            _ret_head(bb, c, hh, *head)


def _ret_layer_call(x, ada, gain, pos, invf, w, gn, dec, qd, kd, chunk_decay, cast_jobs):
    B, S, Dm = x.shape
    ts = RET_STEP_CHUNKS * RET_CHUNK
    tok = lambda n: pl.BlockSpec((RET_BATCH, ts, n), lambda b, s: (b, s, 0))
    const = lambda shape: pl.BlockSpec(shape, lambda b, s: (0,) * len(shape))
    seq_scratch = lambda n: pltpu.VMEM((RET_BATCH, RET_CHUNK, n), BF)
    in_specs = [
        tok(Dm),
        pl.BlockSpec((RET_BATCH, 6, Dm), lambda b, s: (b, 0, 0)),
        const((1, Dm)),
        pl.BlockSpec((RET_BATCH, RET_STEP_CHUNKS, 1, RET_CHUNK), lambda b, s: (b, s, 0, 0)),
        const((1, RET_QK // 2)),
        _resident((None, Dm, 2 * RET_QK_W + 2 * RET_V_W), lambda b, s: (0, 0, 0)),
        const((1, RET_V_W)),
        const((RET_HEADS, RET_CHUNK, RET_CHUNK)),
        const((RET_HEADS, RET_CHUNK, 1)),
        const((RET_HEADS, RET_CHUNK, 1)),
    ]
    (y,), cast = _call_with_casts(
        functools.partial(_ret_layer_kernel, chunk_decay), len(in_specs),
        (B // RET_BATCH, S // ts), in_specs,
        [tok(RET_V_W)], [jax.ShapeDtypeStruct((B, S, RET_V_W), BF)], cast_jobs,
        (x, ada, gain, pos, invf, w, gn, dec, qd, kd),
        scratch_shapes=[
            pltpu.VMEM((RET_BATCH, RET_HEADS, RET_QK, RET_V), F32),
            seq_scratch(RET_QK_W), seq_scratch(RET_QK_W),
            seq_scratch(RET_V_W), seq_scratch(RET_V_W),
        ],
        name="ret_layer")
    return y, cast


def _mlp_kernel(x_ref, y_ref, ada_ref, gain_ref, wo_ref, w1_ref, w2_ref, o_ref, hid_ref):
    ada = ada_ref[0]
    mod = gain_ref[...] * (1.0 + ada[4:5])
    fc = 1024
    for sub in range(MLP_SUB):
        rows = slice(sub * MLP_TM, (sub + 1) * MLP_TM)
        x1 = x_ref[0, rows, :] + ada[2:3] * _dot(y_ref[0, rows, :], wo_ref[...])
        xn = x1 * lax.rsqrt(jnp.mean(x1 * x1, axis=-1, keepdims=True) + EPS)
        h = (xn * mod + ada[3:4]).astype(BF)
        for c in range(D_FF // fc):
            a = jnp.maximum(_dot(h, w1_ref[:, c * fc:(c + 1) * fc]), 0.0)
            hid_ref[rows, c * fc:(c + 1) * fc] = (a * a).astype(BF)
        o_ref[0, rows, :] = x1 + ada[5:6] * _dot(hid_ref[rows, :], w2_ref[...])


MLP_TM = 256
MLP_SUB = 2


def _mlp_call(x, y, ada, gain, wo, w1, w2, name, cast_jobs):
    B, S, Dm = x.shape
    Ky = y.shape[-1]
    tm = MLP_TM * MLP_SUB
    tok = lambda n: pl.BlockSpec((1, tm, n), lambda b, s: (b, s, 0))
    in_specs = [
        tok(Dm), tok(Ky),
        pl.BlockSpec((1, 6, Dm), lambda b, s: (b, 0, 0)),
        pl.BlockSpec((1, Dm), lambda b, s: (0, 0)),
        _resident((Ky, Dm), lambda b, s: (0, 0)),
        _resident((Dm, D_FF), lambda b, s: (0, 0)),
        _resident((D_FF, Dm), lambda b, s: (0, 0)),
    ]
    (out,), cast = _call_with_casts(
        _mlp_kernel, len(in_specs), (B, S // tm), in_specs,
        [tok(Dm)], [jax.ShapeDtypeStruct((B, S, Dm), F32)], cast_jobs,
        (x, y, ada, gain, wo, w1, w2),
        scratch_shapes=[pltpu.VMEM((tm, D_FF), BF)],
        name=name)
    return out, cast


def _group_ones(n, group):
    r = lax.broadcasted_iota(jnp.int32, (n, n), 0) // group
    c = lax.broadcasted_iota(jnp.int32, (n, n), 1) // group
    return jnp.where(r == c, 1.0, 0.0).astype(BF)


def _head_rms(a, ones_blk):
    ss = _dot((a * a).astype(BF), ones_blk)
    return lax.rsqrt(ss * (1.0 / FOX_DH) + EPS)


def _log_sigmoid(z):
    return jnp.minimum(z, 0.0) - jnp.log(1.0 + jnp.exp(-jnp.abs(z)))


ATT_T = 256
VT_PAD = 16
VT_ROWS = FOX_DH + VT_PAD


KVQ_SUB = 2
KVQ_STEP = KVQ_SUB * ATT_T


def _kvq_kernel(x_ref, adakv_ref, ada_ref, gkv_ref, gq_ref, wkv_ref, wq_ref, wft_ref, fb3_ref,
                kg_ref, qg_ref,
                k_ref, kb_ref, vt_ref, f2t_ref, q_ref, sog_ref, carry_row, wf3_ref):
    @pl.when(pl.program_id(1) == 0)
    def _():
        carry_row[...] = jnp.zeros_like(carry_row)

    @pl.when(jnp.logical_and(pl.program_id(0) == 0, pl.program_id(1) == 0))
    def _():
        dst = lax.broadcasted_iota(jnp.int32, (LANES, FOX_HEADS), 0)
        src = lax.broadcasted_iota(jnp.int32, (LANES, FOX_HEADS), 1)
        same_head = lax.bitwise_and(dst, FOX_HEADS - 1) == src
        place = jnp.where(dst < BIAS_ONE0, jnp.where(same_head, 1.0, 0.0), 0.0)
        wf3_ref[...] = _dot(place, wft_ref[...]).T.astype(BF)

    tm = ATT_T
    adakv = adakv_ref[0]
    ada = ada_ref[0]
    mod_kv = gkv_ref[...] * (1.0 + adakv[1:2])
    mod_q = gq_ref[...] * (1.0 + ada[1:2])
    ones_blk = _group_ones(MXU_DIM, FOX_DH)
    qscale = (FOX_DH ** -0.5) * LOG2E
    ri = lax.broadcasted_iota(jnp.int32, (tm, tm), 0)
    ci = lax.broadcasted_iota(jnp.int32, (tm, tm), 1)
    lower = jnp.where(ci <= ri, 1.0, 0.0).astype(BF)
    lane = lax.broadcasted_iota(jnp.int32, (tm, LANES), 1)
    ones_lane = jnp.where(lane < BIAS_ONE0 + 3, 1.0, 0.0)
    pad_rows = jnp.where(lax.broadcasted_iota(jnp.int32, (VT_PAD, tm), 0) == 0, 1.0, 0.0).astype(BF)

    for sub in range(KVQ_SUB):
        rows = slice(sub * tm, (sub + 1) * tm)
        x = x_ref[0, rows, :]
        xn = x * lax.rsqrt(jnp.mean(x * x, axis=-1, keepdims=True) + EPS)
        h = (xn * mod_kv + adakv[0:1]).astype(BF)
        hq = (xn * mod_q + ada[0:1]).astype(BF)

        kv = _dot(h, wkv_ref[...])
        for sl in range(D_MODEL // MXU_DIM):
            cs = slice(sl * MXU_DIM, (sl + 1) * MXU_DIM)
            a = kv[:, cs]
            k_ref[0, rows, cs] = (a * _head_rms(a, ones_blk) * kg_ref[:, cs]).astype(BF)
        for c in range(2):
            vt = kv[:, D_MODEL + c * 512: D_MODEL + (c + 1) * 512].T.astype(BF)
            for i in range(512 // FOX_DH):
                r0 = (c * (512 // FOX_DH) + i) * VT_ROWS
                vt_ref[0, sub, r0:r0 + FOX_DH, :] = vt[i * FOX_DH:(i + 1) * FOX_DH]
                vt_ref[0, sub, r0 + FOX_DH:r0 + VT_ROWS, :] = pad_rows

        hi, mid, lo = _split3(_log_sigmoid(_dot(h, wf3_ref[...]) + fb3_ref[...]))
        cum = ((_dot(lower, hi.astype(BF)) + _dot(lower, mid.astype(BF)))
               + _dot(lower, lo.astype(BF)) + carry_row[...])
        carry_row[...] = cum[tm - 1:tm, :]
        f2 = cum * LOG2E
        hi, mid, lo = _split3(f2)
        kb = jnp.where(lane < FOX_HEADS, -hi,
                       jnp.where(lane < 2 * FOX_HEADS, -mid,
                                 jnp.where(lane < BIAS_ONE0, -lo, ones_lane)))
        kb_ref[0, rows, :] = kb.astype(BF)
        f2t_ref[0, :, rows] = f2.T[:FOX_HEADS, :]

        qo = _dot(hq, wq_ref[...])
        for sl in range(D_MODEL // MXU_DIM):
            cs = slice(sl * MXU_DIM, (sl + 1) * MXU_DIM)
            a = qo[:, cs]
            q_ref[0, rows, cs] = (a * _head_rms(a, ones_blk) * (qg_ref[:, cs] * qscale)).astype(BF)
        sog_ref[0, rows, :] = jax.nn.sigmoid(qo[:, D_MODEL:]).astype(BF)


def _kvq_call(x, ada_kv, ada, gain_kv, gain_q, wkv, wq, kv_wt, fb3, kg, qg, cast_jobs):
    B, S, Dm = x.shape
    ts = KVQ_STEP
    tok = lambda n: pl.BlockSpec((1, ts, n), lambda b, s: (b, s, 0))
    const = lambda shape: pl.BlockSpec(shape, lambda b, s: (0,) * len(shape))
    in_specs = [
        tok(Dm),
        pl.BlockSpec((1, 2, Dm), lambda b, s: (b, 0, 0)),
        pl.BlockSpec((1, 6, Dm), lambda b, s: (b, 0, 0)),
        const((1, Dm)), const((1, Dm)),
        _resident((Dm, 2 * Dm), lambda b, s: (0, 0)),
        _resident((Dm, 2 * Dm), lambda b, s: (0, 0)),
        pl.BlockSpec((FOX_HEADS, Dm), lambda b, s: (2 * Dm // FOX_HEADS, 0)),
        const((1, LANES)),
        const((1, Dm)), const((1, Dm)),
    ]
    out_specs = [
        tok(Dm), tok(LANES),
        pl.BlockSpec((1, KVQ_SUB, FOX_HEADS * VT_ROWS, ATT_T), lambda b, s: (b, s, 0, 0)),
        pl.BlockSpec((1, FOX_HEADS, ts), lambda b, s: (b, 0, s)),
        tok(Dm), tok(Dm),
    ]
    out_shape = [
        jax.ShapeDtypeStruct((B, S, Dm), BF),
        jax.ShapeDtypeStruct((B, S, LANES), BF),
        jax.ShapeDtypeStruct((B, S // ATT_T, FOX_HEADS * VT_ROWS, ATT_T), BF),
        jax.ShapeDtypeStruct((B, FOX_HEADS, S), F32),
        jax.ShapeDtypeStruct((B, S, Dm), BF),
        jax.ShapeDtypeStruct((B, S, Dm), BF),
    ]
    return _call_with_casts(
        _kvq_kernel, len(in_specs), (B, S // ts), in_specs, out_specs, out_shape, cast_jobs,
        (x, ada_kv, ada, gain_kv, gain_q, wkv, wq, kv_wt, fb3, kg, qg),
        scratch_shapes=[pltpu.VMEM((1, LANES), F32),
                        pltpu.VMEM((Dm, LANES), BF)],
        name="kvq")


ATT_SCORE_LEAD = 2


def _attn_scores(j, hd, slot, k_ref, kb_ref, qa_ref, s_ref, bm_ref):
    t = ATT_T
    off = pl.multiple_of(j * t, t)
    p = hd // 2
    kcat = jnp.concatenate([k_ref[0, pl.ds(off, t), p * LANES:(p + 1) * LANES],
                            kb_ref[0, pl.ds(off, t), :]], axis=1)
    s = _dot(kcat, qa_ref[hd])
    s_ref[slot, hd] = s
    bm_ref[slot, hd:hd + 1, :] = jnp.max(s, axis=0, keepdims=True)


def _attn_update(j, hd, slot, masked, vt_ref, s_ref, bm_ref, m_ref, acc_ref):
    t = ATT_T
    s = s_ref[slot, hd]
    m = m_ref[hd:hd + 1, :]
    if masked:
        causal = (lax.broadcasted_iota(jnp.int32, (t, t), 0)
                  <= lax.broadcasted_iota(jnp.int32, (t, t), 1))
        s = jnp.where(causal, s, NEG_BIG)
        m_new = jnp.maximum(m, jnp.max(s, axis=0, keepdims=True))
    else:
        m_new = jnp.maximum(m, bm_ref[slot, hd:hd + 1, :])
    pr = jnp.exp2(s - m_new)
    alpha = jnp.exp2(m - m_new)
    m_ref[hd:hd + 1, :] = m_new
    vth = vt_ref[0, j, hd * VT_ROWS:(hd + 1) * VT_ROWS, :]
    acc_ref[hd] = alpha * acc_ref[hd] + _dot(vth, pr.astype(BF))


def _attn_kernel(q_ref, k_ref, kb_ref, vt_ref, f2t_ref, sog_ref, y_ref,
                 qa_ref, s_ref, bm_ref, m_ref, acc_ref):
    t = ATT_T
    qi = pl.program_id(1)

    @pl.when(jnp.logical_and(pl.program_id(0) == 0, qi == 0))
    def _():
        row = lax.broadcasted_iota(jnp.int32, (2 * LANES, t), 0) - LANES
        for hd in range(FOX_HEADS):
            sel = (jnp.where(row == hd, 1.0, 0.0)
                   + jnp.where(row == FOX_HEADS + hd, 1.0, 0.0)
                   + jnp.where(row == 2 * FOX_HEADS + hd, 1.0, 0.0))
            qa_ref[hd] = sel.astype(BF)

    f_row0 = LANES + BIAS_ONE0
    f_pad = jnp.zeros((VT_PAD - 3, t), F32)
    for p in range(FOX_HEADS // 2):
        cs = slice(p * LANES, (p + 1) * LANES)
        qt = q_ref[0, :, cs].T
        for hh in range(2):
            hd = 2 * p + hh
            qa_ref[hd, hh * FOX_DH:(hh + 1) * FOX_DH, :] = qt[hh * FOX_DH:(hh + 1) * FOX_DH]
            fhi, fmid, flo = _split3(f2t_ref[0, hd:hd + 1, :])
            qa_ref[hd, f_row0:f_row0 + VT_PAD, :] = jnp.concatenate(
                [fhi, fmid, flo, f_pad], axis=0).astype(BF)
            _attn_scores(0, hd, 0, k_ref, kb_ref, qa_ref, s_ref, bm_ref)

    m_ref[...] = jnp.full(m_ref.shape, NEG_BIG, F32)
    acc_ref[...] = jnp.zeros_like(acc_ref)

    def stage(j, cur):
        for g in range(FOX_HEADS + ATT_SCORE_LEAD):
            if g < FOX_HEADS:
                _attn_scores(j + 1, g, 1 - cur, k_ref, kb_ref, qa_ref, s_ref, bm_ref)
            if g >= ATT_SCORE_LEAD:
                _attn_update(j, g - ATT_SCORE_LEAD, cur, False, vt_ref, s_ref, bm_ref, m_ref,
                             acc_ref)

    def last(cur):
        for p in range(FOX_HEADS // 2):
            halves = []
            for hd in (2 * p, 2 * p + 1):
                _attn_update(qi, hd, cur, True, vt_ref, s_ref, bm_ref, m_ref, acc_ref)
                num = acc_ref[hd, :FOX_DH, :]
                den = acc_ref[hd, FOX_DH:FOX_DH + 1, :]
                halves.append(num * (1.0 / den))
            cs = slice(p * LANES, (p + 1) * LANES)
            ot = jnp.concatenate(halves, axis=0)
            y_ref[0, :, cs] = (sog_ref[0, :, cs].astype(F32) * ot.T).astype(BF)

    def body(i, carry):
        stage(2 * i, 0)
        stage(2 * i + 1, 1)
        return carry

    lax.fori_loop(0, lax.shift_right_logical(qi, 1), body, 0)
    odd = lax.bitwise_and(qi, 1) == 1

    @pl.when(odd)
    def _():
        stage(qi - 1, 0)
        last(1)

    @pl.when(jnp.logical_not(odd))
    def _():
        last(0)


def _attn_call(q, k, kb, vt, f2t, sog):
    B, S, Dm = q.shape
    t = ATT_T
    tok = lambda n: pl.BlockSpec((1, t, n), lambda b, s: (b, s, 0))
    return pl.pallas_call(
        _attn_kernel,
        grid=(B, S // t),
        in_specs=[
            tok(Dm),
            pl.BlockSpec((1, S, Dm), lambda b, s: (b, 0, 0)),
            pl.BlockSpec((1, S, LANES), lambda b, s: (b, 0, 0)),
            pl.BlockSpec((1, S // t, FOX_HEADS * VT_ROWS, t), lambda b, s: (b, 0, 0, 0)),
            pl.BlockSpec((1, FOX_HEADS, t), lambda b, s: (b, 0, s)),
            tok(Dm),
        ],
        out_specs=tok(Dm),
        out_shape=jax.ShapeDtypeStruct((B, S, Dm), BF),
        scratch_shapes=[
            pltpu.VMEM((FOX_HEADS, 2 * LANES, t), BF),
            pltpu.VMEM((2, FOX_HEADS, t, t), F32),
            pltpu.VMEM((2, FOX_HEADS, t), F32),
            pltpu.VMEM((FOX_HEADS, t), F32),
            pltpu.VMEM((FOX_HEADS, VT_ROWS, t), F32),
        ],
        compiler_params=_params(("arbitrary", "arbitrary")),
        name="fox_attn",
    )(q, k, kb, vt, f2t, sog)


def _retention_constants():
    gam = 1.0 - np.power(2.0, -5.0 - np.arange(RET_HEADS, dtype=np.float64))
    log_g = np.log(gam)
    idx = np.arange(RET_CHUNK, dtype=np.float64)
    rel = idx[:, None] - idx[None, :]
    dec = np.where(rel >= 0, np.exp(log_g[:, None, None] * np.maximum(rel, 0.0)), 0.0)
    qd = np.exp(log_g[:, None] * (idx + 1.0))[:, :, None]
    kd = np.exp(log_g[:, None] * (RET_CHUNK - 1.0 - idx))[:, :, None]
    chunk_decay = tuple(float(v) for v in np.exp(log_g * RET_CHUNK))
    return (jnp.asarray(dec, F32), jnp.asarray(qd, F32), jnp.asarray(kd, F32), chunk_decay)


def kernel(x, c, positions, norm_mix_gain, norm_mlp_gain, w_ada, b_ada, w_mlp_in, w_mlp_out,
           ret_w_in, ret_norm_gain, ret_w_out, kv_norm_gain, kv_w_ada, kv_b_ada, kv_w,
           forget_bias, k_norm_gain, fox_w_in, q_norm_gain, fox_w_out):
    B, S, Dm = x.shape

    ada = _ada_call(c, w_ada, b_ada[:, None, :])
    ada = ada.reshape(2, B, 6, Dm)
    ada_kv = _ada_call(c, kv_w_ada[None], kv_b_ada[None, None, :])
    ada_kv = ada_kv.reshape(B, 2, Dm)

    rwi = ret_w_in.astype(BF)

    half = RET_QK // 2
    invf = (ROPE_BASE ** (-jnp.arange(half, dtype=F32) / half))[None, :]
    dec, qd, kd, chunk_decay = _retention_constants()
    y, (w1, w2, rwo) = _ret_layer_call(
        x, ada[0], norm_mix_gain[0:1], positions.reshape(B, S // RET_CHUNK, 1, RET_CHUNK), invf, rwi,
        ret_norm_gain[0].reshape(1, RET_V_W), dec, qd, kd, chunk_decay,
        [_rows_job(w_mlp_in, 0), _rows_job(w_mlp_out, 0), _rows_job(ret_w_out, 0)])
    kv_wt = kv_w.T
    x, (kvw, fwi) = _mlp_call(
        x, y, ada[0], norm_mlp_gain[0:1], rwo, w1, w2, "mlp0",
        [_kv_proj_job(kv_wt), _rows_job(fox_w_in, 0)])

    fb3 = jnp.pad(jnp.tile(forget_bias, 3), (0, LANES - BIAS_ONE0))[None, :]
    kg = jnp.tile(k_norm_gain, FOX_HEADS)[None, :]
    qg = jnp.tile(q_norm_gain[0], FOX_HEADS)[None, :]
    (ks, kb, vt, f2t, qf, sog), (w1, w2, fwo) = _kvq_call(
        x, ada_kv, ada[1], kv_norm_gain[None, :], norm_mix_gain[1:2], kvw, fwi, kv_wt, fb3, kg, qg,
        [_rows_job(w_mlp_in, 1), _rows_job(w_mlp_out, 1), _rows_job(fox_w_out, 0)])
    y = _attn_call(qf, ks, kb, vt, f2t, sog)
    x, _ = _mlp_call(x, y, ada[1], norm_mlp_gain[1:2], fwo, w1, w2, "mlp1", [])
    return x
```

```python
import functools
import math

import jax
import jax.numpy as jnp
import numpy as np
from jax import lax
from jax.experimental import pallas as pl
from jax.experimental.pallas import tpu as pltpu

F32 = jnp.float32
BF = jnp.bfloat16

D_MODEL = 1024
RET_HEADS = 4
RET_QK = 256
RET_V = 512
RET_CHUNK = 256
FOX_HEADS = 16
FOX_DH = 64
D_FF = 4096
EPS = 1e-6
ROPE_BASE = 10000.0
LOG2E = math.log2(math.e)
NEG_BIG = -1e30
LANES = 128
MXU_DIM = 256
VMEM_LIMIT = 48 * 1024 * 1024

BIAS_ONE0 = 3 * FOX_HEADS


def _params(sem):
    return pltpu.CompilerParams(dimension_semantics=sem, vmem_limit_bytes=VMEM_LIMIT)


def _resident(shape, index_map):
    return pl.BlockSpec(shape, index_map, pipeline_mode=pl.Buffered(1))


def _ln_mod(x, gain, shift, scale):
    ms = jnp.mean(x * x, axis=-1, keepdims=True)
    y = x * lax.rsqrt(ms + EPS)
    return (y * gain) * (1.0 + scale) + shift


def _dot(a, b):
    return jnp.dot(a, b, preferred_element_type=F32)


def _dot_nt(a, b):
    return lax.dot_general(a, b, (((1,), (1,)), ((), ())), preferred_element_type=F32)


def _split3(a):
    hi = a.astype(BF).astype(F32)
    r1 = a - hi
    mid = r1.astype(BF).astype(F32)
    lo = (r1 - mid).astype(BF).astype(F32)
    return hi, mid, lo


def _rows_job(w, layer=None):
    def build(nsteps, lin):
        R, C = w.shape[-2:]
        rows = R // nsteps
        if layer is None:
            in_spec = pl.BlockSpec((rows, C), lambda *g: (lin(*g), 0))
        else:
            in_spec = pl.BlockSpec((None, rows, C), lambda *g: (layer, lin(*g), 0))
        out_spec = pl.BlockSpec((rows, C), lambda *g: (lin(*g), 0))
        return (w, in_spec, [out_spec], [jax.ShapeDtypeStruct((R, C), BF)],
                lambda v: (v.astype(BF),))
    return build


def _kv_proj_job(kv_wt):
    def build(nsteps, lin):
        Dm = kv_wt.shape[1]
        last = 2 * Dm // LANES - 1
        blk = lambda *g: jnp.minimum(lin(*g), last)
        return (kv_wt,
                pl.BlockSpec((LANES, Dm), lambda *g: (blk(*g), 0)),
                [pl.BlockSpec((Dm, LANES), lambda *g: (0, blk(*g)))],
                [jax.ShapeDtypeStruct((Dm, 2 * Dm), BF)],
                lambda v: (v.T.astype(BF),))
    return build


def _with_casts(body, n_in, n_out, fns, n_cast_out):
    n_cast = len(fns)

    def kernel(*refs):
        ins = refs[:n_in]
        cast_in = refs[n_in:n_in + n_cast]
        outs = refs[n_in + n_cast:n_in + n_cast + n_out]
        first_scratch = n_in + n_cast + n_out + n_cast_out
        cast_out = list(refs[n_in + n_cast + n_out:first_scratch])
        body(*ins, *outs, *refs[first_scratch:])
        for src, fn in zip(cast_in, fns):
            for val in fn(src[...]):
                cast_out.pop(0)[...] = val
    return kernel


def _call_with_casts(body, n_in, grid, in_specs, out_specs, out_shape, jobs, args, **kw):
    nsteps = grid[0] * grid[1]
    lin = lambda b, s: b * grid[1] + s
    built = [job(nsteps, lin) for job in jobs]
    out_specs = list(out_specs)
    out_shape = list(out_shape)
    cast_specs = [s for b in built for s in b[2]]
    cast_shapes = [s for b in built for s in b[3]]
    res = pl.pallas_call(
        _with_casts(body, n_in, len(out_specs), [b[4] for b in built], len(cast_specs)),
        grid=grid,
        in_specs=list(in_specs) + [b[1] for b in built],
        out_specs=out_specs + cast_specs,
        out_shape=out_shape + cast_shapes,
        compiler_params=_params(("arbitrary", "arbitrary")),
        **kw,
    )(*args, *[b[0] for b in built])
    n_out = len(out_specs)
    return list(res[:n_out]), list(res[n_out:])


def _ada_kernel(c_ref, w_ref, b_ref, o_ref):
    c = c_ref[...]
    ca = c * jax.nn.sigmoid(c)
    o_ref[...] = _dot(ca.astype(BF), w_ref[...].astype(BF)) + b_ref[...]


def _ada_call(c, w, b):
    L, Dm, N = w.shape
    B = c.shape[0]
    tn = 2048
    return pl.pallas_call(
        _ada_kernel,
        grid=(L, N // tn),
        in_specs=[
            pl.BlockSpec((B, Dm), lambda l, n: (0, 0)),
            pl.BlockSpec((None, Dm, tn), lambda l, n: (l, 0, n)),
            pl.BlockSpec((None, 1, tn), lambda l, n: (l, 0, n)),
        ],
        out_specs=pl.BlockSpec((None, B, tn), lambda l, n: (l, 0, n)),
        out_shape=jax.ShapeDtypeStruct((L, B, N), F32),
        compiler_params=_params(("arbitrary", "arbitrary")),
        name="ada",
    )(c, w, b)


RET_BATCH = 2
RET_STEP_CHUNKS = 1
RET_QK_W = RET_HEADS * RET_QK
RET_V_W = RET_HEADS * RET_V


def _ret_project_pieces(bb, c, x_ref, ada_ref, gain_ref, pos_ref, invf_ref, w_ref,
                        q_s, k_s, v_s, sg_s):
    rows = slice(c * RET_CHUNK, (c + 1) * RET_CHUNK)
    half = RET_QK // 2
    ctx = {}

    def prepare():
        ada = ada_ref[bb]
        x = x_ref[bb, rows, :]
        xn = x * lax.rsqrt(jnp.mean(x * x, axis=-1, keepdims=True) + EPS)
        ctx["h"] = (xn * (gain_ref[...] * (1.0 + ada[1:2])) + ada[0:1]).astype(BF)
        pos_col = jnp.broadcast_to(pos_ref[bb, c].astype(F32), (RET_QK // 2, RET_CHUNK)).T
        ang = pos_col * invf_ref[...]
        ctx["cos"] = jnp.cos(ang)
        ctx["sin"] = jnp.sin(ang)

    def rotary(dst, base, mul, h0, h1):
        qk = _dot(ctx["h"], w_ref[:, base + h0 * RET_QK: base + h1 * RET_QK])
        cos, sin = ctx["cos"], ctx["sin"]
        for i in range(h1 - h0):
            lo = i * RET_QK
            out = (h0 + i) * RET_QK
            x1 = qk[:, lo: lo + half]
            x2 = qk[:, lo + half: lo + RET_QK]
            dst[bb, :, out: out + half] = ((x1 * cos - x2 * sin) * mul).astype(BF)
            dst[bb, :, out + half: out + RET_QK] = ((x1 * sin + x2 * cos) * mul).astype(BF)

    def p_v(h0, h1):
        base = 2 * RET_QK_W
        v_s[bb, :, h0 * RET_V: h1 * RET_V] = _dot(
            ctx["h"], w_ref[:, base + h0 * RET_V: base + h1 * RET_V]).astype(BF)

    def p_g(h0, h1):
        base = 2 * RET_QK_W + RET_V_W
        g = _dot(ctx["h"], w_ref[:, base + h0 * RET_V: base + h1 * RET_V])
        sg_s[bb, :, h0 * RET_V: h1 * RET_V] = (g * jax.nn.sigmoid(g)).astype(BF)

    def p_q(h0, h1):
        rotary(q_s, 0, 1.0, h0, h1)

    def p_k(h0, h1):
        rotary(k_s, RET_QK_W, RET_QK ** -0.5, h0, h1)

    return prepare, p_v, p_g, p_q, p_k


def _ret_head(bb, c, hh, chunk_decay, q_s, k_s, v_s, sg_s, gn_ref, dec_ref, qd_ref, kd_ref,
              y_ref, state_ref):
    rows = slice(c * RET_CHUNK, (c + 1) * RET_CHUNK)
    qs = slice(hh * RET_QK, (hh + 1) * RET_QK)
    vs = slice(hh * RET_V, (hh + 1) * RET_V)
    q = q_s[bb, :, qs]
    k = k_s[bb, :, qs]
    v = v_s[bb, :, vs]
    s = _dot_nt(q, k) * dec_ref[hh]
    intra = _dot(s.astype(BF), v)
    st = state_ref[bb, hh]
    cross = _dot(q, st.astype(BF)) * qd_ref[hh]
    kdt = (k.astype(F32) * kd_ref[hh]).T.astype(BF)
    state_ref[bb, hh] = st * chunk_decay[hh] + _dot(kdt, v)
    y = intra + cross
    ms = jnp.mean(y * y, axis=-1, keepdims=True)
    yn = ((y * lax.rsqrt(ms + EPS)) * gn_ref[:, vs]).astype(BF)
    y_ref[bb, rows, vs] = sg_s[bb, :, vs] * yn


def _ret_layer_kernel(chunk_decay, x_ref, ada_ref, gain_ref, pos_ref, invf_ref, w_ref,
                      gn_ref, dec_ref, qd_ref, kd_ref, y_ref,
                      state_ref, q_s, k_s, v_s, sg_s):
    @pl.when(pl.program_id(1) == 0)
    def _():
        state_ref[...] = jnp.zeros_like(state_ref)

    proj = (x_ref, ada_ref, gain_ref, pos_ref, invf_ref, w_ref, q_s, k_s, v_s, sg_s)
    head = (chunk_decay, q_s, k_s, v_s, sg_s, gn_ref, dec_ref, qd_ref, kd_ref, y_ref, state_ref)
    tiles = [(bb, c) for c in range(RET_STEP_CHUNKS) for bb in range(RET_BATCH)]
    H = RET_HEADS

    def full_width(tile):
        prep, p_v, p_g, p_q, p_k = _ret_project_pieces(*tile, *proj)

        def first():
            prep()
            p_v(0, H)
        return [first] + [functools.partial(p, 0, H) for p in (p_g, p_q, p_k)]

    for piece in full_width(tiles[0]):
        piece()
    for i, (bb, c) in enumerate(tiles):
        nxt = full_width(tiles[i + 1]) if i + 1 < len(tiles) else [None] * H
        for hh in range(H):
            if nxt[hh] is not None:
                nxt[hh]()
            _ret_head(bb, c, hh, *head)


def _ret_layer_call(x, ada, gain, pos, invf, w, gn, dec, qd, kd, chunk_decay, cast_jobs):
    B, S, Dm = x.shape
    ts = RET_STEP_CHUNKS * RET_CHUNK
    tok = lambda n: pl.BlockSpec((RET_BATCH, ts, n), lambda b, s: (b, s, 0))
    const = lambda shape: pl.BlockSpec(shape, lambda b, s: (0,) * len(shape))
    seq_scratch = lambda n: pltpu.VMEM((RET_BATCH, RET_CHUNK, n), BF)
    in_specs = [
        tok(Dm),
        pl.BlockSpec((RET_BATCH, 6, Dm), lambda b, s: (b, 0, 0)),
        const((1, Dm)),
        pl.BlockSpec((RET_BATCH, RET_STEP_CHUNKS, 1, RET_CHUNK), lambda b, s: (b, s, 0, 0)),
        const((1, RET_QK // 2)),
        _resident((None, Dm, 2 * RET_QK_W + 2 * RET_V_W), lambda b, s: (0, 0, 0)),
        const((1, RET_V_W)),
        const((RET_HEADS, RET_CHUNK, RET_CHUNK)),
        const((RET_HEADS, RET_CHUNK, 1)),
        const((RET_HEADS, RET_CHUNK, 1)),
    ]
    (y,), cast = _call_with_casts(
        functools.partial(_ret_layer_kernel, chunk_decay), len(in_specs),
        (B // RET_BATCH, S // ts), in_specs,
        [tok(RET_V_W)], [jax.ShapeDtypeStruct((B, S, RET_V_W), BF)], cast_jobs,
        (x, ada, gain, pos, invf, w, gn, dec, qd, kd),
        scratch_shapes=[
            pltpu.VMEM((RET_BATCH, RET_HEADS, RET_QK, RET_V), F32),
            seq_scratch(RET_QK_W), seq_scratch(RET_QK_W),
            seq_scratch(RET_V_W), seq_scratch(RET_V_W),
        ],
        name="ret_layer")
    return y, cast


def _mlp_kernel(x_ref, y_ref, ada_ref, gain_ref, wo_ref, w1_ref, w2_ref, o_ref, hid_ref):
    ada = ada_ref[0]
    mod = gain_ref[...] * (1.0 + ada[4:5])
    fc = 1024
    for sub in range(MLP_SUB):
        rows = slice(sub * MLP_TM, (sub + 1) * MLP_TM)
        x1 = x_ref[0, rows, :] + ada[2:3] * _dot(y_ref[0, rows, :], wo_ref[...])
        xn = x1 * lax.rsqrt(jnp.mean(x1 * x1, axis=-1, keepdims=True) + EPS)
        h = (xn * mod + ada[3:4]).astype(BF)
        for c in range(D_FF // fc):
            a = jnp.maximum(_dot(h, w1_ref[:, c * fc:(c + 1) * fc]), 0.0)
            hid_ref[rows, c * fc:(c + 1) * fc] = (a * a).astype(BF)
        o_ref[0, rows, :] = x1 + ada[5:6] * _dot(hid_ref[rows, :], w2_ref[...])


MLP_TM = 256
MLP_SUB = 2


def _mlp_call(x, y, ada, gain, wo, w1, w2, name, cast_jobs):
    B, S, Dm = x.shape
    Ky = y.shape[-1]
    tm = MLP_TM * MLP_SUB
    tok = lambda n: pl.BlockSpec((1, tm, n), lambda b, s: (b, s, 0))
    in_specs = [
        tok(Dm), tok(Ky),
        pl.BlockSpec((1, 6, Dm), lambda b, s: (b, 0, 0)),
        pl.BlockSpec((1, Dm), lambda b, s: (0, 0)),
        _resident((Ky, Dm), lambda b, s: (0, 0)),
        _resident((Dm, D_FF), lambda b, s: (0, 0)),
        _resident((D_FF, Dm), lambda b, s: (0, 0)),
    ]
    (out,), cast = _call_with_casts(
        _mlp_kernel, len(in_specs), (B, S // tm), in_specs,
        [tok(Dm)], [jax.ShapeDtypeStruct((B, S, Dm), F32)], cast_jobs,
        (x, y, ada, gain, wo, w1, w2),
        scratch_shapes=[pltpu.VMEM((tm, D_FF), BF)],
        name=name)
    return out, cast


def _group_ones(n, group):
    r = lax.broadcasted_iota(jnp.int32, (n, n), 0) // group
    c = lax.broadcasted_iota(jnp.int32, (n, n), 1) // group
    return jnp.where(r == c, 1.0, 0.0).astype(BF)


def _head_rms(a, ones_blk):
    ss = _dot((a * a).astype(BF), ones_blk)
    return lax.rsqrt(ss * (1.0 / FOX_DH) + EPS)


def _log_sigmoid(z):
    return jnp.minimum(z, 0.0) - jnp.log(1.0 + jnp.exp(-jnp.abs(z)))


ATT_T = 256
VT_PAD = 16
VT_ROWS = FOX_DH + VT_PAD


KVQ_SUB = 2
KVQ_STEP = KVQ_SUB * ATT_T


def _kvq_kernel(x_ref, adakv_ref, ada_ref, gkv_ref, gq_ref, wkv_ref, wq_ref, wft_ref, fb3_ref,
                kg_ref, qg_ref,
                k_ref, kb_ref, vt_ref, f2t_ref, q_ref, sog_ref, carry_row, wf3_ref):
    @pl.when(pl.program_id(1) == 0)
    def _():
        carry_row[...] = jnp.zeros_like(carry_row)

    @pl.when(jnp.logical_and(pl.program_id(0) == 0, pl.program_id(1) == 0))
    def _():
        dst = lax.broadcasted_iota(jnp.int32, (LANES, FOX_HEADS), 0)
        src = lax.broadcasted_iota(jnp.int32, (LANES, FOX_HEADS), 1)
        same_head = lax.bitwise_and(dst, FOX_HEADS - 1) == src
        place = jnp.where(dst < BIAS_ONE0, jnp.where(same_head, 1.0, 0.0), 0.0)
        wf3_ref[...] = _dot(place, wft_ref[...]).T.astype(BF)

    tm = ATT_T
    adakv = adakv_ref[0]
    ada = ada_ref[0]
    mod_kv = gkv_ref[...] * (1.0 + adakv[1:2])
    mod_q = gq_ref[...] * (1.0 + ada[1:2])
    ones_blk = _group_ones(MXU_DIM, FOX_DH)
    qscale = (FOX_DH ** -0.5) * LOG2E
    ri = lax.broadcasted_iota(jnp.int32, (tm, tm), 0)
    ci = lax.broadcasted_iota(jnp.int32, (tm, tm), 1)
    lower = jnp.where(ci <= ri, 1.0, 0.0).astype(BF)
    lane = lax.broadcasted_iota(jnp.int32, (tm, LANES), 1)
    ones_lane = jnp.where(lane < BIAS_ONE0 + 3, 1.0, 0.0)
    pad_rows = jnp.where(lax.broadcasted_iota(jnp.int32, (VT_PAD, tm), 0) == 0, 1.0, 0.0).astype(BF)

    for sub in range(KVQ_SUB):
        rows = slice(sub * tm, (sub + 1) * tm)
        x = x_ref[0, rows, :]
        xn = x * lax.rsqrt(jnp.mean(x * x, axis=-1, keepdims=True) + EPS)
        h = (xn * mod_kv + adakv[0:1]).astype(BF)
        hq = (xn * mod_q + ada[0:1]).astype(BF)

        kv = _dot(h, wkv_ref[...])
        for sl in range(D_MODEL // MXU_DIM):
            cs = slice(sl * MXU_DIM, (sl + 1) * MXU_DIM)
            a = kv[:, cs]
            k_ref[0, rows, cs] = (a * _head_rms(a, ones_blk) * kg_ref[:, cs]).astype(BF)
        for c in range(2):
            vt = kv[:, D_MODEL + c * 512: D_MODEL + (c + 1) * 512].T.astype(BF)
            for i in range(512 // FOX_DH):
                r0 = (c * (512 // FOX_DH) + i) * VT_ROWS
                vt_ref[0, sub, r0:r0 + FOX_DH, :] = vt[i * FOX_DH:(i + 1) * FOX_DH]
                vt_ref[0, sub, r0 + FOX_DH:r0 + VT_ROWS, :] = pad_rows

        hi, mid, lo = _split3(_log_sigmoid(_dot(h, wf3_ref[...]) + fb3_ref[...]))
        cum = ((_dot(lower, hi.astype(BF)) + _dot(lower, mid.astype(BF)))
               + _dot(lower, lo.astype(BF)) + carry_row[...])
        carry_row[...] = cum[tm - 1:tm, :]
        f2 = cum * LOG2E
        hi, mid, lo = _split3(f2)
        kb = jnp.where(lane < FOX_HEADS, -hi,
                       jnp.where(lane < 2 * FOX_HEADS, -mid,
                                 jnp.where(lane < BIAS_ONE0, -lo, ones_lane)))
        kb_ref[0, rows, :] = kb.astype(BF)
        f2t_ref[0, :, rows] = f2.T[:FOX_HEADS, :]

        qo = _dot(hq, wq_ref[...])
        for sl in range(D_MODEL // MXU_DIM):
            cs = slice(sl * MXU_DIM, (sl + 1) * MXU_DIM)
            a = qo[:, cs]
            q_ref[0, rows, cs] = (a * _head_rms(a, ones_blk) * (qg_ref[:, cs] * qscale)).astype(BF)
        sog_ref[0, rows, :] = jax.nn.sigmoid(qo[:, D_MODEL:]).astype(BF)


def _kvq_call(x, ada_kv, ada, gain_kv, gain_q, wkv, wq, kv_wt, fb3, kg, qg, cast_jobs):
    B, S, Dm = x.shape
    ts = KVQ_STEP
    tok = lambda n: pl.BlockSpec((1, ts, n), lambda b, s: (b, s, 0))
    const = lambda shape: pl.BlockSpec(shape, lambda b, s: (0,) * len(shape))
    in_specs = [
        tok(Dm),
        pl.BlockSpec((1, 2, Dm), lambda b, s: (b, 0, 0)),
        pl.BlockSpec((1, 6, Dm), lambda b, s: (b, 0, 0)),
        const((1, Dm)), const((1, Dm)),
        _resident((Dm, 2 * Dm), lambda b, s: (0, 0)),
        _resident((Dm, 2 * Dm), lambda b, s: (0, 0)),
        pl.BlockSpec((FOX_HEADS, Dm), lambda b, s: (2 * Dm // FOX_HEADS, 0)),
        const((1, LANES)),
        const((1, Dm)), const((1, Dm)),
    ]
    out_specs = [
        tok(Dm), tok(LANES),
        pl.BlockSpec((1, KVQ_SUB, FOX_HEADS * VT_ROWS, ATT_T), lambda b, s: (b, s, 0, 0)),
        pl.BlockSpec((1, FOX_HEADS, ts), lambda b, s: (b, 0, s)),
        tok(Dm), tok(Dm),
    ]
    out_shape = [
        jax.ShapeDtypeStruct((B, S, Dm), BF),
        jax.ShapeDtypeStruct((B, S, LANES), BF),
        jax.ShapeDtypeStruct((B, S // ATT_T, FOX_HEADS * VT_ROWS, ATT_T), BF),
        jax.ShapeDtypeStruct((B, FOX_HEADS, S), F32),
        jax.ShapeDtypeStruct((B, S, Dm), BF),
        jax.ShapeDtypeStruct((B, S, Dm), BF),
    ]
    return _call_with_casts(
        _kvq_kernel, len(in_specs), (B, S // ts), in_specs, out_specs, out_shape, cast_jobs,
        (x, ada_kv, ada, gain_kv, gain_q, wkv, wq, kv_wt, fb3, kg, qg),
        scratch_shapes=[pltpu.VMEM((1, LANES), F32),
                        pltpu.VMEM((Dm, LANES), BF)],
        name="kvq")


ATT_SCORE_LEAD = 2


def _attn_scores(j, hd, slot, k_ref, kb_ref, qa_ref, s_ref, bm_ref):
    t = ATT_T
    off = pl.multiple_of(j * t, t)
    p = hd // 2
    kcat = jnp.concatenate([k_ref[0, pl.ds(off, t), p * LANES:(p + 1) * LANES],
                            kb_ref[0, pl.ds(off, t), :]], axis=1)
    s = _dot(kcat, qa_ref[hd])
    s_ref[slot, hd] = s
    bm_ref[slot, hd:hd + 1, :] = jnp.max(s, axis=0, keepdims=True)


def _attn_update(j, hd, slot, masked, vt_ref, s_ref, bm_ref, m_ref, acc_ref):
    t = ATT_T
    s = s_ref[slot, hd]
    m = m_ref[hd:hd + 1, :]
    if masked:
        causal = (lax.broadcasted_iota(jnp.int32, (t, t), 0)
                  <= lax.broadcasted_iota(jnp.int32, (t, t), 1))
        s = jnp.where(causal, s, NEG_BIG)
        m_new = jnp.maximum(m, jnp.max(s, axis=0, keepdims=True))
    else:
        m_new = jnp.maximum(m, bm_ref[slot, hd:hd + 1, :])
    pr = jnp.exp2(s - m_new)
    alpha = jnp.exp2(m - m_new)
    m_ref[hd:hd + 1, :] = m_new
    vth = vt_ref[0, j, hd * VT_ROWS:(hd + 1) * VT_ROWS, :]
    acc_ref[hd] = alpha * acc_ref[hd] + _dot(vth, pr.astype(BF))


def _attn_kernel(q_ref, k_ref, kb_ref, vt_ref, f2t_ref, sog_ref, y_ref,
                 qa_ref, s_ref, bm_ref, m_ref, acc_ref):
    t = ATT_T
    qi = pl.program_id(1)

    @pl.when(jnp.logical_and(pl.program_id(0) == 0, qi == 0))
    def _():
        row = lax.broadcasted_iota(jnp.int32, (2 * LANES, t), 0) - LANES
        for hd in range(FOX_HEADS):
            sel = (jnp.where(row == hd, 1.0, 0.0)
                   + jnp.where(row == FOX_HEADS + hd, 1.0, 0.0)
                   + jnp.where(row == 2 * FOX_HEADS + hd, 1.0, 0.0))
            qa_ref[hd] = sel.astype(BF)

    f_row0 = LANES + BIAS_ONE0
    f_pad = jnp.zeros((VT_PAD - 3, t), F32)
    for p in range(FOX_HEADS // 2):
        cs = slice(p * LANES, (p + 1) * LANES)
        qt = q_ref[0, :, cs].T
        for hh in range(2):
            hd = 2 * p + hh
            qa_ref[hd, hh * FOX_DH:(hh + 1) * FOX_DH, :] = qt[hh * FOX_DH:(hh + 1) * FOX_DH]
            fhi, fmid, flo = _split3(f2t_ref[0, hd:hd + 1, :])
            qa_ref[hd, f_row0:f_row0 + VT_PAD, :] = jnp.concatenate(
                [fhi, fmid, flo, f_pad], axis=0).astype(BF)
            _attn_scores(0, hd, 0, k_ref, kb_ref, qa_ref, s_ref, bm_ref)

    m_ref[...] = jnp.full(m_ref.shape, NEG_BIG, F32)
    acc_ref[...] = jnp.zeros_like(acc_ref)

    def stage(j, cur):
        for g in range(FOX_HEADS + ATT_SCORE_LEAD):
            if g < FOX_HEADS:
                _attn_scores(j + 1, g, 1 - cur, k_ref, kb_ref, qa_ref, s_ref, bm_ref)
            if g >= ATT_SCORE_LEAD:
                _attn_update(j, g - ATT_SCORE_LEAD, cur, False, vt_ref, s_ref, bm_ref, m_ref,
                             acc_ref)

    def last(cur):
        for p in range(FOX_HEADS // 2):
            halves = []
            for hd in (2 * p, 2 * p + 1):
                _attn_update(qi, hd, cur, True, vt_ref, s_ref, bm_ref, m_ref, acc_ref)
                num = acc_ref[hd, :FOX_DH, :]
                den = acc_ref[hd, FOX_DH:FOX_DH + 1, :]
                halves.append(num * (1.0 / den))
            cs = slice(p * LANES, (p + 1) * LANES)
            ot = jnp.concatenate(halves, axis=0)
            y_ref[0, :, cs] = (sog_ref[0, :, cs].astype(F32) * ot.T).astype(BF)

    def body(i, carry):
        stage(2 * i, 0)
        stage(2 * i + 1, 1)
        return carry

    lax.fori_loop(0, lax.shift_right_logical(qi, 1), body, 0)
    odd = lax.bitwise_and(qi, 1) == 1

    @pl.when(odd)
    def _():
        stage(qi - 1, 0)
        last(1)

    @pl.when(jnp.logical_not(odd))
    def _():
        last(0)


def _attn_call(q, k, kb, vt, f2t, sog):
    B, S, Dm = q.shape
    t = ATT_T
    tok = lambda n: pl.BlockSpec((1, t, n), lambda b, s: (b, s, 0))
    return pl.pallas_call(
        _attn_kernel,
        grid=(B, S // t),
        in_specs=[
            tok(Dm),
            pl.BlockSpec((1, S, Dm), lambda b, s: (b, 0, 0)),
            pl.BlockSpec((1, S, LANES), lambda b, s: (b, 0, 0)),
            pl.BlockSpec((1, S // t, FOX_HEADS * VT_ROWS, t), lambda b, s: (b, 0, 0, 0)),
            pl.BlockSpec((1, FOX_HEADS, t), lambda b, s: (b, 0, s)),
            tok(Dm),
        ],
        out_specs=tok(Dm),
        out_shape=jax.ShapeDtypeStruct((B, S, Dm), BF),
        scratch_shapes=[
            pltpu.VMEM((FOX_HEADS, 2 * LANES, t), BF),
            pltpu.VMEM((2, FOX_HEADS, t, t), F32),
            pltpu.VMEM((2, FOX_HEADS, t), F32),
            pltpu.VMEM((FOX_HEADS, t), F32),
            pltpu.VMEM((FOX_HEADS, VT_ROWS, t), F32),
        ],
        compiler_params=_params(("arbitrary", "arbitrary")),
        name="fox_attn",
    )(q, k, kb, vt, f2t, sog)


def _retention_constants():
    gam = 1.0 - np.power(2.0, -5.0 - np.arange(RET_HEADS, dtype=np.float64))
    log_g = np.log(gam)
    idx = np.arange(RET_CHUNK, dtype=np.float64)
    rel = idx[:, None] - idx[None, :]
    dec = np.where(rel >= 0, np.exp(log_g[:, None, None] * np.maximum(rel, 0.0)), 0.0)
    qd = np.exp(log_g[:, None] * (idx + 1.0))[:, :, None]
    kd = np.exp(log_g[:, None] * (RET_CHUNK - 1.0 - idx))[:, :, None]
    chunk_decay = tuple(float(v) for v in np.exp(log_g * RET_CHUNK))
    return (jnp.asarray(dec, F32), jnp.asarray(qd, F32), jnp.asarray(kd, F32), chunk_decay)


def kernel(x, c, positions, norm_mix_gain, norm_mlp_gain, w_ada, b_ada, w_mlp_in, w_mlp_out,
           ret_w_in, ret_norm_gain, ret_w_out, kv_norm_gain, kv_w_ada, kv_b_ada, kv_w,
           forget_bias, k_norm_gain, fox_w_in, q_norm_gain, fox_w_out):
    B, S, Dm = x.shape

    ada = _ada_call(c, w_ada, b_ada[:, None, :])
    ada = ada.reshape(2, B, 6, Dm)
    ada_kv = _ada_call(c, kv_w_ada[None], kv_b_ada[None, None, :])
    ada_kv = ada_kv.reshape(B, 2, Dm)

    rwi = ret_w_in.astype(BF)

    half = RET_QK // 2
    invf = (ROPE_BASE ** (-jnp.arange(half, dtype=F32) / half))[None, :]
    dec, qd, kd, chunk_decay = _retention_constants()
    y, (w1, w2, rwo) = _ret_layer_call(
        x, ada[0], norm_mix_gain[0:1], positions.reshape(B, S // RET_CHUNK, 1, RET_CHUNK), invf, rwi,
        ret_norm_gain[0].reshape(1, RET_V_W), dec, qd, kd, chunk_decay,
        [_rows_job(w_mlp_in, 0), _rows_job(w_mlp_out, 0), _rows_job(ret_w_out, 0)])
    kv_wt = kv_w.T
    x, (kvw, fwi) = _mlp_call(
        x, y, ada[0], norm_mlp_gain[0:1], rwo, w1, w2, "mlp0",
        [_kv_proj_job(kv_wt), _rows_job(fox_w_in, 0)])

    fb3 = jnp.pad(jnp.tile(forget_bias, 3), (0, LANES - BIAS_ONE0))[None, :]
    kg = jnp.tile(k_norm_gain, FOX_HEADS)[None, :]
    qg = jnp.tile(q_norm_gain[0], FOX_HEADS)[None, :]
    (ks, kb, vt, f2t, qf, sog), (w1, w2, fwo) = _kvq_call(
        x, ada_kv, ada[1], kv_norm_gain[None, :], norm_mix_gain[1:2], kvw, fwi, kv_wt, fb3, kg, qg,
        [_rows_job(w_mlp_in, 1), _rows_job(w_mlp_out, 1), _rows_job(fox_w_out, 0)])
    y = _attn_call(qf, ks, kb, vt, f2t, sog)
    x, _ = _mlp_call(x, y, ada[1], norm_mlp_gain[1:2], fwo, w1, w2, "mlp1", [])
    return x
```

```python
import functools
import math

import jax
import jax.numpy as jnp
import numpy as np
from jax import lax
from jax.experimental import pallas as pl
from jax.experimental.pallas import tpu as pltpu

F32 = jnp.float32
BF = jnp.bfloat16

D_MODEL = 1024
RET_HEADS = 4
RET_QK = 256
RET_V = 512
RET_CHUNK = 256
FOX_HEADS = 16
FOX_DH = 64
D_FF = 4096
EPS = 1e-6
ROPE_BASE = 10000.0
LOG2E = math.log2(math.e)
NEG_BIG = -1e30
LANES = 128
MXU_DIM = 256
VMEM_LIMIT = 48 * 1024 * 1024

BIAS_ONE0 = 3 * FOX_HEADS


def _params(sem):
    return pltpu.CompilerParams(dimension_semantics=sem, vmem_limit_bytes=VMEM_LIMIT)


def _resident(shape, index_map):
    return pl.BlockSpec(shape, index_map, pipeline_mode=pl.Buffered(1))


def _dot(a, b):
    return jnp.dot(a, b, preferred_element_type=F32)


def _dot_nt(a, b):
    return lax.dot_general(a, b, (((1,), (1,)), ((), ())), preferred_element_type=F32)


def _split3(a):
    hi = a.astype(BF).astype(F32)
    r1 = a - hi
    mid = r1.astype(BF).astype(F32)
    lo = (r1 - mid).astype(BF).astype(F32)
    return hi, mid, lo


def _rows_job(w, layer=None):
    def build(nsteps, lin):
        R, C = w.shape[-2:]
        rows = R // nsteps
        if layer is None:
            in_spec = pl.BlockSpec((rows, C), lambda *g: (lin(*g), 0))
        else:
            in_spec = pl.BlockSpec((None, rows, C), lambda *g: (layer, lin(*g), 0))
        out_spec = pl.BlockSpec((rows, C), lambda *g: (lin(*g), 0))
        return (w, in_spec, [out_spec], [jax.ShapeDtypeStruct((R, C), BF)],
                lambda v: (v.astype(BF),))
    return build


def _kv_proj_job(kv_wt):
    def build(nsteps, lin):
        Dm = kv_wt.shape[1]
        last = 2 * Dm // LANES - 1
        blk = lambda *g: jnp.minimum(lin(*g), last)
        return (kv_wt,
                pl.BlockSpec((LANES, Dm), lambda *g: (blk(*g), 0)),
                [pl.BlockSpec((Dm, LANES), lambda *g: (0, blk(*g)))],
                [jax.ShapeDtypeStruct((Dm, 2 * Dm), BF)],
                lambda v: (v.T.astype(BF),))
    return build


def _with_casts(body, n_in, n_out, fns, n_cast_out):
    n_cast = len(fns)

    def kernel(*refs):
        ins = refs[:n_in]
        cast_in = refs[n_in:n_in + n_cast]
        outs = refs[n_in + n_cast:n_in + n_cast + n_out]
        first_scratch = n_in + n_cast + n_out + n_cast_out
        cast_out = list(refs[n_in + n_cast + n_out:first_scratch])
        body(*ins, *outs, *refs[first_scratch:])
        for src, fn in zip(cast_in, fns):
            for val in fn(src[...]):
                cast_out.pop(0)[...] = val
    return kernel


def _call_with_casts(body, n_in, grid, in_specs, out_specs, out_shape, jobs, args, **kw):
    nsteps = grid[0] * grid[1]
    lin = lambda b, s: b * grid[1] + s
    built = [job(nsteps, lin) for job in jobs]
    out_specs = list(out_specs)
    out_shape = list(out_shape)
    cast_specs = [s for b in built for s in b[2]]
    cast_shapes = [s for b in built for s in b[3]]
    res = pl.pallas_call(
        _with_casts(body, n_in, len(out_specs), [b[4] for b in built], len(cast_specs)),
        grid=grid,
        in_specs=list(in_specs) + [b[1] for b in built],
        out_specs=out_specs + cast_specs,
        out_shape=out_shape + cast_shapes,
        compiler_params=_params(("arbitrary", "arbitrary")),
        **kw,
    )(*args, *[b[0] for b in built])
    n_out = len(out_specs)
    return list(res[:n_out]), list(res[n_out:])


def _ada_kernel(c_ref, w_ref, b_ref, o_ref):
    c = c_ref[...]
    ca = c * jax.nn.sigmoid(c)
    o_ref[...] = _dot(ca.astype(BF), w_ref[...].astype(BF)) + b_ref[...]


def _ada_call(c, w, b):
    L, Dm, N = w.shape
    B = c.shape[0]
    tn = 2048
    return pl.pallas_call(
        _ada_kernel,
        grid=(L, N // tn),
        in_specs=[
            pl.BlockSpec((B, Dm), lambda l, n: (0, 0)),
            pl.BlockSpec((None, Dm, tn), lambda l, n: (l, 0, n)),
            pl.BlockSpec((None, 1, tn), lambda l, n: (l, 0, n)),
        ],
        out_specs=pl.BlockSpec((None, B, tn), lambda l, n: (l, 0, n)),
        out_shape=jax.ShapeDtypeStruct((L, B, N), F32),
        compiler_params=_params(("arbitrary", "arbitrary")),
        name="ada",
    )(c, w, b)


RET_BATCH = 2
RET_STEP_CHUNKS = 1
RET_QK_W = RET_HEADS * RET_QK
RET_V_W = RET_HEADS * RET_V


def _ret_project_pieces(bb, c, x_ref, ada_ref, gain_ref, pos_ref, invf_ref, w_ref,
                        q_s, k_s, v_s, sg_s):
    rows = slice(c * RET_CHUNK, (c + 1) * RET_CHUNK)
    half = RET_QK // 2
    ctx = {}

    def prepare():
        ada = ada_ref[bb]
        x = x_ref[bb, rows, :]
        xn = x * lax.rsqrt(jnp.mean(x * x, axis=-1, keepdims=True) + EPS)
        ctx["h"] = (xn * (gain_ref[...] * (1.0 + ada[1:2])) + ada[0:1]).astype(BF)
        pos_col = jnp.broadcast_to(pos_ref[bb, c].astype(F32), (RET_QK // 2, RET_CHUNK)).T
        ang = pos_col * invf_ref[...]
        ctx["cos"] = jnp.cos(ang)
        ctx["sin"] = jnp.sin(ang)

    def rotary(dst, base, mul, h0, h1):
        qk = _dot(ctx["h"], w_ref[:, base + h0 * RET_QK: base + h1 * RET_QK])
        cos, sin = ctx["cos"], ctx["sin"]
        for i in range(h1 - h0):
            lo = i * RET_QK
            out = (h0 + i) * RET_QK
            x1 = qk[:, lo: lo + half]
            x2 = qk[:, lo + half: lo + RET_QK]
            dst[bb, :, out: out + half] = ((x1 * cos - x2 * sin) * mul).astype(BF)
            dst[bb, :, out + half: out + RET_QK] = ((x1 * sin + x2 * cos) * mul).astype(BF)

    def p_v(h0, h1):
        base = 2 * RET_QK_W
        v_s[bb, :, h0 * RET_V: h1 * RET_V] = _dot(
            ctx["h"], w_ref[:, base + h0 * RET_V: base + h1 * RET_V]).astype(BF)

    def p_g(h0, h1):
        base = 2 * RET_QK_W + RET_V_W
        g = _dot(ctx["h"], w_ref[:, base + h0 * RET_V: base + h1 * RET_V])
        sg_s[bb, :, h0 * RET_V: h1 * RET_V] = (g * jax.nn.sigmoid(g)).astype(BF)

    def p_q(h0, h1):
        rotary(q_s, 0, 1.0, h0, h1)

    def p_k(h0, h1):
        rotary(k_s, RET_QK_W, RET_QK ** -0.5, h0, h1)

    return prepare, p_v, p_g, p_q, p_k


def _ret_head(bb, c, hh, chunk_decay, q_s, k_s, v_s, sg_s, gn_ref, dec_ref, qd_ref, kd_ref,
              y_ref, state_ref):
    rows = slice(c * RET_CHUNK, (c + 1) * RET_CHUNK)
    qs = slice(hh * RET_QK, (hh + 1) * RET_QK)
    vs = slice(hh * RET_V, (hh + 1) * RET_V)
    q = q_s[bb, :, qs]
    k = k_s[bb, :, qs]
    v = v_s[bb, :, vs]
    s = _dot_nt(q, k) * dec_ref[hh]
    intra = _dot(s.astype(BF), v)
    st = state_ref[bb, hh]
    cross = _dot(q, st.astype(BF)) * qd_ref[hh]
    kdt = (k.astype(F32) * kd_ref[hh]).T.astype(BF)
    state_ref[bb, hh] = st * chunk_decay[hh] + _dot(kdt, v)
    y = intra + cross
    ms = jnp.mean(y * y, axis=-1, keepdims=True)
    yn = ((y * lax.rsqrt(ms + EPS)) * gn_ref[:, vs]).astype(BF)
    y_ref[bb, rows, vs] = sg_s[bb, :, vs] * yn


def _ret_layer_kernel(chunk_decay, x_ref, ada_ref, gain_ref, pos_ref, invf_ref, w_ref,
                      gn_ref, dec_ref, qd_ref, kd_ref, y_ref,
                      state_ref, q_s, k_s, v_s, sg_s):
    @pl.when(pl.program_id(1) == 0)
    def _():
        state_ref[...] = jnp.zeros_like(state_ref)

    proj = (x_ref, ada_ref, gain_ref, pos_ref, invf_ref, w_ref, q_s, k_s, v_s, sg_s)
    head = (chunk_decay, q_s, k_s, v_s, sg_s, gn_ref, dec_ref, qd_ref, kd_ref, y_ref, state_ref)
    tiles = [(bb, c) for c in range(RET_STEP_CHUNKS) for bb in range(RET_BATCH)]
    H = RET_HEADS

    def full_width(tile):
        prep, p_v, p_g, p_q, p_k = _ret_project_pieces(*tile, *proj)

        def first():
            prep()
            p_v(0, H)
        return [first] + [functools.partial(p, 0, H) for p in (p_g, p_q, p_k)]

    for piece in full_width(tiles[0]):
        piece()
    for i, (bb, c) in enumerate(tiles):
        nxt = full_width(tiles[i + 1]) if i + 1 < len(tiles) else [None] * H
        for hh in range(H):
            if nxt[hh] is not None:
                nxt[hh]()
            _ret_head(bb, c, hh, *head)


def _ret_layer_call(x, ada, gain, pos, invf, w, gn, dec, qd, kd, chunk_decay, cast_jobs):
    B, S, Dm = x.shape
    ts = RET_STEP_CHUNKS * RET_CHUNK
    tok = lambda n: pl.BlockSpec((RET_BATCH, ts, n), lambda b, s: (b, s, 0))
    const = lambda shape: pl.BlockSpec(shape, lambda b, s: (0,) * len(shape))
    seq_scratch = lambda n: pltpu.VMEM((RET_BATCH, RET_CHUNK, n), BF)
    in_specs = [
        tok(Dm),
        pl.BlockSpec((RET_BATCH, 6, Dm), lambda b, s: (b, 0, 0)),
        const((1, Dm)),
        pl.BlockSpec((RET_BATCH, RET_STEP_CHUNKS, 1, RET_CHUNK), lambda b, s: (b, s, 0, 0)),
        const((1, RET_QK // 2)),
        _resident((None, Dm, 2 * RET_QK_W + 2 * RET_V_W), lambda b, s: (0, 0, 0)),
        const((1, RET_V_W)),
        const((RET_HEADS, RET_CHUNK, RET_CHUNK)),
        const((RET_HEADS, RET_CHUNK, 1)),
        const((RET_HEADS, RET_CHUNK, 1)),
    ]
    (y,), cast = _call_with_casts(
        functools.partial(_ret_layer_kernel, chunk_decay), len(in_specs),
        (B // RET_BATCH, S // ts), in_specs,
        [tok(RET_V_W)], [jax.ShapeDtypeStruct((B, S, RET_V_W), BF)], cast_jobs,
        (x, ada, gain, pos, invf, w, gn, dec, qd, kd),
        scratch_shapes=[
            pltpu.VMEM((RET_BATCH, RET_HEADS, RET_QK, RET_V), F32),
            seq_scratch(RET_QK_W), seq_scratch(RET_QK_W),
            seq_scratch(RET_V_W), seq_scratch(RET_V_W),
        ],
        name="ret_layer")
    return y, cast


def _mlp_kernel(x_ref, y_ref, ada_ref, gain_ref, wo_ref, w1_ref, w2_ref, o_ref, hid_ref):
    ada = ada_ref[0]
    mod = gain_ref[...] * (1.0 + ada[4:5])
    fc = 1024
    for sub in range(MLP_SUB):
        rows = slice(sub * MLP_TM, (sub + 1) * MLP_TM)
        x1 = x_ref[0, rows, :] + ada[2:3] * _dot(y_ref[0, rows, :], wo_ref[...])
        xn = x1 * lax.rsqrt(jnp.mean(x1 * x1, axis=-1, keepdims=True) + EPS)
        h = (xn * mod + ada[3:4]).astype(BF)
        for c in range(D_FF // fc):
            a = jnp.maximum(_dot(h, w1_ref[:, c * fc:(c + 1) * fc]), 0.0)
            hid_ref[rows, c * fc:(c + 1) * fc] = (a * a).astype(BF)
        o_ref[0, rows, :] = x1 + ada[5:6] * _dot(hid_ref[rows, :], w2_ref[...])


MLP_TM = 256
MLP_SUB = 2


def _mlp_call(x, y, ada, gain, wo, w1, w2, name, cast_jobs):
    B, S, Dm = x.shape
    Ky = y.shape[-1]
    tm = MLP_TM * MLP_SUB
    tok = lambda n: pl.BlockSpec((1, tm, n), lambda b, s: (b, s, 0))
    in_specs = [
        tok(Dm), tok(Ky),
        pl.BlockSpec((1, 6, Dm), lambda b, s: (b, 0, 0)),
        pl.BlockSpec((1, Dm), lambda b, s: (0, 0)),
        _resident((Ky, Dm), lambda b, s: (0, 0)),
        _resident((Dm, D_FF), lambda b, s: (0, 0)),
        _resident((D_FF, Dm), lambda b, s: (0, 0)),
    ]
    (out,), cast = _call_with_casts(
        _mlp_kernel, len(in_specs), (B, S // tm), in_specs,
        [tok(Dm)], [jax.ShapeDtypeStruct((B, S, Dm), F32)], cast_jobs,
        (x, y, ada, gain, wo, w1, w2),
        scratch_shapes=[pltpu.VMEM((tm, D_FF), BF)],
        name=name)
    return out, cast


def _group_ones(n, group):
    r = lax.broadcasted_iota(jnp.int32, (n, n), 0) // group
    c = lax.broadcasted_iota(jnp.int32, (n, n), 1) // group
    return jnp.where(r == c, 1.0, 0.0).astype(BF)


def _head_rms(a, ones_blk):
    ss = _dot((a * a).astype(BF), ones_blk)
    return lax.rsqrt(ss * (1.0 / FOX_DH) + EPS)


def _log_sigmoid(z):
    return jnp.minimum(z, 0.0) - jnp.log(1.0 + jnp.exp(-jnp.abs(z)))


ATT_T = 256
VT_PAD = 16
VT_ROWS = FOX_DH + VT_PAD


KVQ_SUB = 2
KVQ_STEP = KVQ_SUB * ATT_T


def _kvq_kernel(x_ref, adakv_ref, ada_ref, gkv_ref, gq_ref, wkv_ref, wq_ref, wft_ref, fb3_ref,
                kg_ref, qg_ref,
                k_ref, kb_ref, vt_ref, f2t_ref, q_ref, sog_ref, carry_row, wf3_ref):
    @pl.when(pl.program_id(1) == 0)
    def _():
        carry_row[...] = jnp.zeros_like(carry_row)

    @pl.when(jnp.logical_and(pl.program_id(0) == 0, pl.program_id(1) == 0))
    def _():
        dst = lax.broadcasted_iota(jnp.int32, (LANES, FOX_HEADS), 0)
        src = lax.broadcasted_iota(jnp.int32, (LANES, FOX_HEADS), 1)
        same_head = lax.bitwise_and(dst, FOX_HEADS - 1) == src
        place = jnp.where(dst < BIAS_ONE0, jnp.where(same_head, 1.0, 0.0), 0.0)
        wf3_ref[...] = _dot(place, wft_ref[...]).T.astype(BF)

    tm = ATT_T
    adakv = adakv_ref[0]
    ada = ada_ref[0]
    mod_kv = gkv_ref[...] * (1.0 + adakv[1:2])
    mod_q = gq_ref[...] * (1.0 + ada[1:2])
    ones_blk = _group_ones(MXU_DIM, FOX_DH)
    qscale = (FOX_DH ** -0.5) * LOG2E
    ri = lax.broadcasted_iota(jnp.int32, (tm, tm), 0)
    ci = lax.broadcasted_iota(jnp.int32, (tm, tm), 1)
    lower = jnp.where(ci <= ri, 1.0, 0.0).astype(BF)
    lane = lax.broadcasted_iota(jnp.int32, (tm, LANES), 1)
    ones_lane = jnp.where(lane < BIAS_ONE0 + 3, 1.0, 0.0)
    pad_rows = jnp.where(lax.broadcasted_iota(jnp.int32, (VT_PAD, tm), 0) == 0, 1.0, 0.0).astype(BF)

    for sub in range(KVQ_SUB):
        rows = slice(sub * tm, (sub + 1) * tm)
        x = x_ref[0, rows, :]
        xn = x * lax.rsqrt(jnp.mean(x * x, axis=-1, keepdims=True) + EPS)
        h = (xn * mod_kv + adakv[0:1]).astype(BF)
        hq = (xn * mod_q + ada[0:1]).astype(BF)

        kv = _dot(h, wkv_ref[...])
        for sl in range(D_MODEL // MXU_DIM):
            cs = slice(sl * MXU_DIM, (sl + 1) * MXU_DIM)
            a = kv[:, cs]
            k_ref[0, rows, cs] = (a * _head_rms(a, ones_blk) * kg_ref[:, cs]).astype(BF)
        for c in range(2):
            vt = kv[:, D_MODEL + c * 512: D_MODEL + (c + 1) * 512].T.astype(BF)
            for i in range(512 // FOX_DH):
                r0 = (c * (512 // FOX_DH) + i) * VT_ROWS
                vt_ref[0, sub, r0:r0 + FOX_DH, :] = vt[i * FOX_DH:(i + 1) * FOX_DH]
                vt_ref[0, sub, r0 + FOX_DH:r0 + VT_ROWS, :] = pad_rows

        hi, mid, lo = _split3(_log_sigmoid(_dot(h, wf3_ref[...]) + fb3_ref[...]))
        cum = ((_dot(lower, hi.astype(BF)) + _dot(lower, mid.astype(BF)))
               + _dot(lower, lo.astype(BF)) + carry_row[...])
        carry_row[...] = cum[tm - 1:tm, :]
        f2 = cum * LOG2E
        hi, mid, lo = _split3(f2)
        kb = jnp.where(lane < FOX_HEADS, -hi,
                       jnp.where(lane < 2 * FOX_HEADS, -mid,
                                 jnp.where(lane < BIAS_ONE0, -lo, ones_lane)))
        kb_ref[0, rows, :] = kb.astype(BF)
        f2t_ref[0, :, rows] = f2.T[:FOX_HEADS, :]

        qo = _dot(hq, wq_ref[...])
        for sl in range(D_MODEL // MXU_DIM):
            cs = slice(sl * MXU_DIM, (sl + 1) * MXU_DIM)
            a = qo[:, cs]
            q_ref[0, rows, cs] = (a * _head_rms(a, ones_blk) * (qg_ref[:, cs] * qscale)).astype(BF)
        sog_ref[0, rows, :] = jax.nn.sigmoid(qo[:, D_MODEL:]).astype(BF)


def _kvq_call(x, ada_kv, ada, gain_kv, gain_q, wkv, wq, kv_wt, fb3, kg, qg, cast_jobs):
    B, S, Dm = x.shape
    ts = KVQ_STEP
    tok = lambda n: pl.BlockSpec((1, ts, n), lambda b, s: (b, s, 0))
    const = lambda shape: pl.BlockSpec(shape, lambda b, s: (0,) * len(shape))
    in_specs = [
        tok(Dm),
        pl.BlockSpec((1, 2, Dm), lambda b, s: (b, 0, 0)),
        pl.BlockSpec((1, 6, Dm), lambda b, s: (b, 0, 0)),
        const((1, Dm)), const((1, Dm)),
        _resident((Dm, 2 * Dm), lambda b, s: (0, 0)),
        _resident((Dm, 2 * Dm), lambda b, s: (0, 0)),
        pl.BlockSpec((FOX_HEADS, Dm), lambda b, s: (2 * Dm // FOX_HEADS, 0)),
        const((1, LANES)),
        const((1, Dm)), const((1, Dm)),
    ]
    out_specs = [
        tok(Dm), tok(LANES),
        pl.BlockSpec((1, KVQ_SUB, FOX_HEADS * VT_ROWS, ATT_T), lambda b, s: (b, s, 0, 0)),
        pl.BlockSpec((1, FOX_HEADS, ts), lambda b, s: (b, 0, s)),
        tok(Dm), tok(Dm),
    ]
    out_shape = [
        jax.ShapeDtypeStruct((B, S, Dm), BF),
        jax.ShapeDtypeStruct((B, S, LANES), BF),
        jax.ShapeDtypeStruct((B, S // ATT_T, FOX_HEADS * VT_ROWS, ATT_T), BF),
        jax.ShapeDtypeStruct((B, FOX_HEADS, S), F32),
        jax.ShapeDtypeStruct((B, S, Dm), BF),
        jax.ShapeDtypeStruct((B, S, Dm), BF),
    ]
    return _call_with_casts(
        _kvq_kernel, len(in_specs), (B, S // ts), in_specs, out_specs, out_shape, cast_jobs,
        (x, ada_kv, ada, gain_kv, gain_q, wkv, wq, kv_wt, fb3, kg, qg),
        scratch_shapes=[pltpu.VMEM((1, LANES), F32),
                        pltpu.VMEM((Dm, LANES), BF)],
        name="kvq")


ATT_SCORE_LEAD = 2


def _attn_scores(j, hd, slot, k_ref, kb_ref, qa_ref, s_ref, bm_ref):
    t = ATT_T
    off = pl.multiple_of(j * t, t)
    p = hd // 2
    kcat = jnp.concatenate([k_ref[0, pl.ds(off, t), p * LANES:(p + 1) * LANES],
                            kb_ref[0, pl.ds(off, t), :]], axis=1)
    s = _dot(kcat, qa_ref[hd])
    s_ref[slot, hd] = s
    bm_ref[slot, hd:hd + 1, :] = jnp.max(s, axis=0, keepdims=True)


def _attn_update(j, hd, slot, masked, vt_ref, s_ref, bm_ref, m_ref, acc_ref):
    t = ATT_T
    s = s_ref[slot, hd]
    m = m_ref[hd:hd + 1, :]
    if masked:
        causal = (lax.broadcasted_iota(jnp.int32, (t, t), 0)
                  <= lax.broadcasted_iota(jnp.int32, (t, t), 1))
        s = jnp.where(causal, s, NEG_BIG)
        m_new = jnp.maximum(m, jnp.max(s, axis=0, keepdims=True))
    else:
        m_new = jnp.maximum(m, bm_ref[slot, hd:hd + 1, :])
    pr = jnp.exp2(s - m_new)
    alpha = jnp.exp2(m - m_new)
    m_ref[hd:hd + 1, :] = m_new
    vth = vt_ref[0, j, hd * VT_ROWS:(hd + 1) * VT_ROWS, :]
    acc_ref[hd] = alpha * acc_ref[hd] + _dot(vth, pr.astype(BF))


def _attn_kernel(q_ref, k_ref, kb_ref, vt_ref, f2t_ref, sog_ref, y_ref,
                 qa_ref, s_ref, bm_ref, m_ref, acc_ref):
    t = ATT_T
    qi = pl.program_id(1)

    @pl.when(jnp.logical_and(pl.program_id(0) == 0, qi == 0))
    def _():
        row = lax.broadcasted_iota(jnp.int32, (2 * LANES, t), 0) - LANES
        for hd in range(FOX_HEADS):
            sel = (jnp.where(row == hd, 1.0, 0.0)
                   + jnp.where(row == FOX_HEADS + hd, 1.0, 0.0)
                   + jnp.where(row == 2 * FOX_HEADS + hd, 1.0, 0.0))
            qa_ref[hd] = sel.astype(BF)

    f_row0 = LANES + BIAS_ONE0
    f_pad = jnp.zeros((VT_PAD - 3, t), F32)
    for p in range(FOX_HEADS // 2):
        cs = slice(p * LANES, (p + 1) * LANES)
        qt = q_ref[0, :, cs].T
        for hh in range(2):
            hd = 2 * p + hh
            qa_ref[hd, hh * FOX_DH:(hh + 1) * FOX_DH, :] = qt[hh * FOX_DH:(hh + 1) * FOX_DH]
            fhi, fmid, flo = _split3(f2t_ref[0, hd:hd + 1, :])
            qa_ref[hd, f_row0:f_row0 + VT_PAD, :] = jnp.concatenate(
                [fhi, fmid, flo, f_pad], axis=0).astype(BF)
            _attn_scores(0, hd, 0, k_ref, kb_ref, qa_ref, s_ref, bm_ref)

    m_ref[...] = jnp.full(m_ref.shape, NEG_BIG, F32)
    acc_ref[...] = jnp.zeros_like(acc_ref)

    def stage(j, cur):
        for g in range(FOX_HEADS + ATT_SCORE_LEAD):
            if g < FOX_HEADS:
                _attn_scores(j + 1, g, 1 - cur, k_ref, kb_ref, qa_ref, s_ref, bm_ref)
            if g >= ATT_SCORE_LEAD:
                _attn_update(j, g - ATT_SCORE_LEAD, cur, False, vt_ref, s_ref, bm_ref, m_ref,
                             acc_ref)

    def last(cur):
        for p in range(FOX_HEADS // 2):
            halves = []
            for hd in (2 * p, 2 * p + 1):
                _attn_update(qi, hd, cur, True, vt_ref, s_ref, bm_ref, m_ref, acc_ref)
                num = acc_ref[hd, :FOX_DH, :]
                den = acc_ref[hd, FOX_DH:FOX_DH + 1, :]
                halves.append(num * (1.0 / den))
            cs = slice(p * LANES, (p + 1) * LANES)
            ot = jnp.concatenate(halves, axis=0)
            y_ref[0, :, cs] = (sog_ref[0, :, cs].astype(F32) * ot.T).astype(BF)

    def body(i, carry):
        stage(2 * i, 0)
        stage(2 * i + 1, 1)
        return carry

    lax.fori_loop(0, lax.shift_right_logical(qi, 1), body, 0)
    odd = lax.bitwise_and(qi, 1) == 1

    @pl.when(odd)
    def _():
        stage(qi - 1, 0)
        last(1)

    @pl.when(jnp.logical_not(odd))
    def _():
        last(0)


def _attn_call(q, k, kb, vt, f2t, sog):
    B, S, Dm = q.shape
    t = ATT_T
    tok = lambda n: pl.BlockSpec((1, t, n), lambda b, s: (b, s, 0))
    return pl.pallas_call(
        _attn_kernel,
        grid=(B, S // t),
        in_specs=[
            tok(Dm),
            pl.BlockSpec((1, S, Dm), lambda b, s: (b, 0, 0)),
            pl.BlockSpec((1, S, LANES), lambda b, s: (b, 0, 0)),
            pl.BlockSpec((1, S // t, FOX_HEADS * VT_ROWS, t), lambda b, s: (b, 0, 0, 0)),
            pl.BlockSpec((1, FOX_HEADS, t), lambda b, s: (b, 0, s)),
            tok(Dm),
        ],
        out_specs=tok(Dm),
        out_shape=jax.ShapeDtypeStruct((B, S, Dm), BF),
        scratch_shapes=[
            pltpu.VMEM((FOX_HEADS, 2 * LANES, t), BF),
            pltpu.VMEM((2, FOX_HEADS, t, t), F32),
            pltpu.VMEM((2, FOX_HEADS, t), F32),
            pltpu.VMEM((FOX_HEADS, t), F32),
            pltpu.VMEM((FOX_HEADS, VT_ROWS, t), F32),
        ],
        compiler_params=_params(("arbitrary", "arbitrary")),
        name="fox_attn",
    )(q, k, kb, vt, f2t, sog)


def _retention_constants():
    gam = 1.0 - np.power(2.0, -5.0 - np.arange(RET_HEADS, dtype=np.float64))
    log_g = np.log(gam)
    idx = np.arange(RET_CHUNK, dtype=np.float64)
    rel = idx[:, None] - idx[None, :]
    dec = np.where(rel >= 0, np.exp(log_g[:, None, None] * np.maximum(rel, 0.0)), 0.0)
    qd = np.exp(log_g[:, None] * (idx + 1.0))[:, :, None]
    kd = np.exp(log_g[:, None] * (RET_CHUNK - 1.0 - idx))[:, :, None]
    chunk_decay = tuple(float(v) for v in np.exp(log_g * RET_CHUNK))
    return (jnp.asarray(dec, F32), jnp.asarray(qd, F32), jnp.asarray(kd, F32), chunk_decay)


def kernel(x, c, positions, norm_mix_gain, norm_mlp_gain, w_ada, b_ada, w_mlp_in, w_mlp_out,
           ret_w_in, ret_norm_gain, ret_w_out, kv_norm_gain, kv_w_ada, kv_b_ada, kv_w,
           forget_bias, k_norm_gain, fox_w_in, q_norm_gain, fox_w_out):
    B, S, Dm = x.shape

    ada = _ada_call(c, w_ada, b_ada[:, None, :])
    ada = ada.reshape(2, B, 6, Dm)
    ada_kv = _ada_call(c, kv_w_ada[None], kv_b_ada[None, None, :])
    ada_kv = ada_kv.reshape(B, 2, Dm)

    rwi = ret_w_in.astype(BF)

    half = RET_QK // 2
    invf = (ROPE_BASE ** (-jnp.arange(half, dtype=F32) / half))[None, :]
    dec, qd, kd, chunk_decay = _retention_constants()
    y, (w1, w2, rwo) = _ret_layer_call(
        x, ada[0], norm_mix_gain[0:1], positions.reshape(B, S // RET_CHUNK, 1, RET_CHUNK), invf, rwi,
        ret_norm_gain[0].reshape(1, RET_V_W), dec, qd, kd, chunk_decay,
        [_rows_job(w_mlp_in, 0), _rows_job(w_mlp_out, 0), _rows_job(ret_w_out, 0)])
    kv_wt = kv_w.T
    x, (kvw, fwi) = _mlp_call(
        x, y, ada[0], norm_mlp_gain[0:1], rwo, w1, w2, "mlp0",
        [_kv_proj_job(kv_wt), _rows_job(fox_w_in, 0)])

    fb3 = jnp.pad(jnp.tile(forget_bias, 3), (0, LANES - BIAS_ONE0))[None, :]
    kg = jnp.tile(k_norm_gain, FOX_HEADS)[None, :]
    qg = jnp.tile(q_norm_gain[0], FOX_HEADS)[None, :]
    (ks, kb, vt, f2t, qf, sog), (w1, w2, fwo) = _kvq_call(
        x, ada_kv, ada[1], kv_norm_gain[None, :], norm_mix_gain[1:2], kvw, fwi, kv_wt, fb3, kg, qg,
        [_rows_job(w_mlp_in, 1), _rows_job(w_mlp_out, 1), _rows_job(fox_w_out, 0)])
    y = _attn_call(qf, ks, kb, vt, f2t, sog)
    x, _ = _mlp_call(x, y, ada[1], norm_mlp_gain[1:2], fwo, w1, w2, "mlp1", [])
    return x
```
